```python
import jax, jax.numpy as jnp
from jax import lax
import numpy as np

D_MODEL = 2048
BATCH = 4
SEQ = 2048
DEPTH = 4

N_META = 16
SB_HEADS = 8
SB_HEAD_DIM = D_MODEL // 16
SB_WIDTH = SB_HEADS * SB_HEAD_DIM
POOL_WINDOWS = (2, 4, 8, 16)
POOL_GROUPS = len(POOL_WINDOWS)
POOL_WIDTH = D_MODEL // 2
POOL_GROUP_DIM = POOL_WIDTH // POOL_GROUPS
Q_BLOCK = 128
RMS_EPS = 1e-6
IN_SIZES = (SB_WIDTH, SB_WIDTH, SB_WIDTH, SB_WIDTH, POOL_WIDTH, POOL_WIDTH, D_MODEL, D_MODEL)
IN_COLS = sum(IN_SIZES)

kernel_name = "hybrid_stickbreak_pool_gated_trunk"


def rmsnorm(x, gain):
    xf = x.astype(jnp.float32)
    y = xf * lax.rsqrt(jnp.mean(xf * xf, axis=-1, keepdims=True) + RMS_EPS)
    return (y * gain.astype(jnp.float32)).astype(x.dtype)


def stick_breaking_block(q_blk, k_pre, v_pre, q0):
    nq = q_blk.shape[1]
    nk = k_pre.shape[1]
    z = jnp.einsum('bqhd,bkhd->bhqk', q_blk.astype(jnp.float32), k_pre.astype(jnp.float32)) * (SB_HEAD_DIM ** -0.5)
    qpos = q0 + jnp.arange(nq)
    kpos = jnp.arange(nk)
    causal = kpos[None, :] < qpos[:, None]
    log_1m_beta = jnp.where(causal, jax.nn.log_sigmoid(-z), 0.0)
    between = lax.cumsum(log_1m_beta, axis=3, reverse=True) - log_1m_beta
    a = jnp.where(causal, jnp.exp(jax.nn.log_sigmoid(z) + between), 0.0)
    o = jnp.einsum('bhqk,bkhd->bqhd', a, v_pre.astype(jnp.float32))
    return o


def stick_breaking_attention(q, k, v):
    L = q.shape[1]
    bounds = [(0, N_META)] + [(s, min(s + Q_BLOCK, L)) for s in range(N_META, L, Q_BLOCK)]
    outs = [stick_breaking_block(q[:, s:e], k[:, :e], v[:, :e], s) for (s, e) in bounds]
    return jnp.concatenate(outs, axis=1)


def multiscale_pool(u):
    L = u.shape[1]
    wmax = max(POOL_WINDOWS)
    c = jnp.cumsum(jnp.pad(u, ((0, 0), (wmax, 0), (0, 0))), axis=1)
    steps = jnp.arange(1, L + 1, dtype=jnp.float32)[None, :, None]
    outs = []
    for g, w in enumerate(POOL_WINDOWS):
        sl = slice(g * POOL_GROUP_DIM, (g + 1) * POOL_GROUP_DIM)
        win_sum = c[:, wmax:, sl] - c[:, wmax - w:wmax - w + L, sl]
        cnt = jnp.minimum(steps, float(w))
        outs.append(win_sum / cnt - u[:, :, sl])
    return jnp.stack(outs, axis=2)


def hybrid_layer(x, gain, w_in, pool_w, pool_scale, w_attn_up, w_pool_up, w_out):
    B, L, _ = x.shape
    h = rmsnorm(x, gain)
    zin = h @ w_in.astype(h.dtype)
    cuts = np.cumsum(IN_SIZES)[:-1].tolist()
    q, k, v, g_attn, u, g_pool, m_attn, m_pool = jnp.split(zin, cuts, axis=-1)
    q = q.reshape(B, L, SB_HEADS, SB_HEAD_DIM)
    k = k.reshape(B, L, SB_HEADS, SB_HEAD_DIM)
    v = v.reshape(B, L, SB_HEADS, SB_HEAD_DIM)
    o_attn = stick_breaking_attention(q, k, v).reshape(B, L, SB_WIDTH)
    o_attn = (o_attn * jax.nn.silu(g_attn.astype(jnp.float32))).astype(x.dtype)
    y_attn = o_attn @ w_attn_up.astype(x.dtype)
    pooled = multiscale_pool(u.astype(jnp.float32))
    mixed = jnp.einsum('blgc,gcd->blgd', pooled, pool_w.astype(jnp.float32)).reshape(B, L, POOL_WIDTH)
    o_pool = (mixed * pool_scale.astype(jnp.float32) * jax.nn.silu(g_pool.astype(jnp.float32))).astype(x.dtype)
    y_pool = o_pool @ w_pool_up.astype(x.dtype)
    merged = jax.nn.sigmoid(m_attn) * y_attn + jax.nn.sigmoid(m_pool) * y_pool
    return x + merged @ w_out.astype(x.dtype)


def setup_inputs(seed: int = 0) -> dict:
    key = jax.random.key(seed)
    ks = jax.random.split(key, 10)
    f32 = jnp.float32
    x = jax.random.normal(ks[0], (BATCH, SEQ, D_MODEL), f32)
    meta_tokens = jax.random.normal(ks[1], (N_META, D_MODEL), f32)
    norm_gain = 1.0 + 0.02 * jax.random.normal(ks[2], (DEPTH, D_MODEL), f32)
    w_in = jax.random.normal(ks[3], (DEPTH, D_MODEL, IN_COLS), f32) * D_MODEL ** -0.5
    pool_w = jax.random.normal(ks[4], (DEPTH, POOL_GROUPS, POOL_GROUP_DIM, POOL_GROUP_DIM), f32) * POOL_GROUP_DIM ** -0.5
    pool_scale = 1.0 + 0.1 * jax.random.normal(ks[5], (DEPTH, POOL_WIDTH), f32)
    w_attn_up = jax.random.normal(ks[6], (DEPTH, SB_WIDTH, D_MODEL), f32) * SB_WIDTH ** -0.5
    w_pool_up = jax.random.normal(ks[7], (DEPTH, POOL_WIDTH, D_MODEL), f32) * POOL_WIDTH ** -0.5
    w_out = jax.random.normal(ks[8], (DEPTH, D_MODEL, D_MODEL), f32) * D_MODEL ** -0.5
    final_gain = 1.0 + 0.02 * jax.random.normal(ks[9], (D_MODEL,), f32)
    return {"x": x, "meta_tokens": meta_tokens, "norm_gain": norm_gain, "w_in": w_in,
            "pool_w": pool_w, "pool_scale": pool_scale, "w_attn_up": w_attn_up,
            "w_pool_up": w_pool_up, "w_out": w_out, "final_gain": final_gain}


def reference(x, meta_tokens, norm_gain, w_in, pool_w, pool_scale, w_attn_up, w_pool_up, w_out, final_gain):
    B = x.shape[0]
    meta = jnp.broadcast_to(meta_tokens.astype(x.dtype)[None], (B, N_META, x.shape[2]))
    hs = jnp.concatenate([meta, x], axis=1)
    for layer in range(DEPTH):
        hs = hybrid_layer(hs, norm_gain[layer], w_in[layer], pool_w[layer], pool_scale[layer],
                          w_attn_up[layer], w_pool_up[layer], w_out[layer])
    return rmsnorm(hs, final_gain)[:, N_META:]
```

```python
import functools

import jax
import jax.numpy as jnp
from jax import lax
from jax.experimental import pallas as pl
from jax.experimental.pallas import tpu as pltpu

N_META = 16
HEADS = 8
HEAD_DIM = 128
POOL_WINDOWS = (2, 4, 8, 16)
RMS_EPS = 1e-6
KEY_BLOCK = 128
ROW_TILE = 688
HALO = 16
IN_TILE = 512
VMEM_LIMIT = 56 * 1024 * 1024

BF16 = jnp.bfloat16
F32 = jnp.float32


def _params(n_axes):
    return pltpu.CompilerParams(
        dimension_semantics=("arbitrary",) * n_axes,
        vmem_limit_bytes=VMEM_LIMIT)


def _sigmoid(x):
    return 1.0 / (1.0 + jnp.exp(-x))


def _rms_scale(x):
    return lax.rsqrt(jnp.mean(x * x, axis=-1, keepdims=True) + RMS_EPS)


def _norm_kernel(x_ref, g_ref, o_ref):
    x = x_ref[0]
    o_ref[0] = (x * _rms_scale(x) * g_ref[0]).astype(o_ref.dtype)


def _prenorm(hs, gains, layer):
    b, l, d = hs.shape
    return pl.pallas_call(
        _norm_kernel,
        grid=(b, l // ROW_TILE),
        in_specs=[pl.BlockSpec((1, ROW_TILE, d), lambda i, r: (i, r, 0)),
                  pl.BlockSpec((1, 1, d), lambda i, r: (layer, 0, 0))],
        out_specs=pl.BlockSpec((1, ROW_TILE, d), lambda i, r: (i, r, 0)),
        out_shape=jax.ShapeDtypeStruct((b, l, d), BF16),
        compiler_params=_params(2),
        name="prenorm",
    )(hs, gains)


def _inproj_kernel(h_ref, w_ref, o_ref):
    w = w_ref[0].astype(BF16)
    o_ref[0] = jnp.dot(h_ref[0], w, preferred_element_type=F32)


def _inproj(h, w_in, layer):
    b, l, d = h.shape
    n = w_in.shape[2]
    return pl.pallas_call(
        _inproj_kernel,
        grid=(b, n // IN_TILE),
        in_specs=[pl.BlockSpec((1, l, d), lambda i, j: (i, 0, 0)),
                  pl.BlockSpec((1, d, IN_TILE), lambda i, j: (layer, 0, j))],
        out_specs=pl.BlockSpec((1, l, IN_TILE), lambda i, j: (i, 0, j)),
        out_shape=jax.ShapeDtypeStruct((b, l, n), F32),
        compiler_params=_params(2),
        name="inproj",
    )(h, w_in)


def _softplus(z):
    return jnp.maximum(z, 0.0) + jnp.log(1.0 + jnp.exp(-jnp.abs(z)))


def _attn_kernel(q_ref, k_ref, v_ref, g_ref, o_ref, kb_ref, vb_ref, *, seq):
    kb_ref[...] = k_ref[0].astype(BF16)
    vb_ref[...] = v_ref[0].astype(BF16)
    scale = HEAD_DIM ** -0.5
    kblk = KEY_BLOCK

    wr = lax.broadcasted_iota(jnp.int32, (2 * kblk, 2 * kblk), 0) & (kblk - 1)
    wc = lax.broadcasted_iota(jnp.int32, (2 * kblk, 2 * kblk), 1)
    w = jnp.where((wc >= kblk) | (wr > wc), 1.0, 0.0).astype(BF16)

    def block(q, k0, c, acc, mask):
        kk = kb_ref[pl.ds(k0, kblk), :]
        z = lax.dot_general(q, kk, (((1,), (1,)), ((), ())),
                            preferred_element_type=F32) * scale
        sp = _softplus(z)
        spm = sp if mask is None else jnp.where(mask, sp, 0.0)
        hi = spm.astype(BF16)
        lo = (spm - hi.astype(F32)).astype(BF16)
        r = jnp.dot(jnp.concatenate([hi, lo], axis=1), w, preferred_element_type=F32)
        between = r[:, :kblk] + c
        a = jnp.exp(z - sp - between)
        if mask is not None:
            a = jnp.where(mask, a, 0.0)
        vv = vb_ref[pl.ds(k0, kblk), :]
        acc = acc + jnp.dot(a.astype(BF16), vv, preferred_element_type=F32)
        return c + r[:, kblk:], acc

    def finish(q0, nq, acc):
        g = g_ref[0, pl.ds(q0, nq), :]
        o_ref[0, pl.ds(q0, nq), :] = (acc * (g / (1.0 + jnp.exp(-g)))).astype(o_ref.dtype)

    def tile(i, _):
        q0 = pl.multiple_of(i * kblk, kblk)
        q = q_ref[0, pl.ds(q0, kblk), :].astype(BF16)
        row = lax.broadcasted_iota(jnp.int32, (kblk, kblk), 0)
        col = lax.broadcasted_iota(jnp.int32, (kblk, kblk), 1)
        zero = jnp.zeros((kblk, kblk), F32)
        c, acc = block(q, q0, zero, zero, col < row)

        def earlier(j, carry):
            k0 = pl.multiple_of((i - 1 - j) * kblk, kblk)
            return block(q, k0, carry[0], carry[1], None)

        c, acc = lax.fori_loop(0, i, earlier, (c, acc))
        finish(q0, kblk, acc)
        return 0

    n_full = seq // kblk
    lax.fori_loop(0, n_full, tile, 0)

    nq = seq - n_full * kblk
    if nq:
        q0 = n_full * kblk
        q = q_ref[0, pl.ds(q0, nq), :].astype(BF16)
        row = lax.broadcasted_iota(jnp.int32, (nq, kblk), 0)
        col = lax.broadcasted_iota(jnp.int32, (nq, kblk), 1)
        zero = jnp.zeros((nq, kblk), F32)
        c, acc = block(q, seq - kblk, zero, zero, col < row + (kblk - nq))

        def earlier_tail(j, carry):
            k0 = pl.multiple_of(seq - (j + 2) * kblk, nq)
            return block(q, k0, carry[0], carry[1], None)

        c, acc = lax.fori_loop(0, n_full - 1, earlier_tail, (c, acc))
        c, acc = block(q, 0, c, acc, col < nq)
        finish(q0, nq, acc)


def _attention(zin):
    b, l, _ = zin.shape
    width = HEADS * HEAD_DIM
    assert l % 16 == 0

    def spec(part):
        return pl.BlockSpec((1, l, HEAD_DIM), lambda i, h: (i, 0, part * HEADS + h))

    return pl.pallas_call(
        functools.partial(_attn_kernel, seq=l),
        grid=(b, HEADS),
        in_specs=[spec(0), spec(1), spec(2), spec(3)],
        out_specs=pl.BlockSpec((1, l, HEAD_DIM), lambda i, h: (i, 0, h)),
        out_shape=jax.ShapeDtypeStruct((b, l, width), BF16),
        scratch_shapes=[pltpu.VMEM((l, HEAD_DIM), BF16), pltpu.VMEM((l, HEAD_DIM), BF16)],
        compiler_params=_params(2),
        name="stickbreak_attn",
    )(zin, zin, zin, zin)


def _merge_kernel(u_ref, uh_ref, gp_ref, og_ref, ma_ref, mp_ref, pw_ref, ps_ref, wa_ref, wp_ref,
                  o_ref, op_ref):
    r = pl.program_id(1)
    n = pl.program_id(2)
    rows = u_ref.shape[1]
    gd = pw_ref.shape[2]

    @pl.when(n == 0)
    def _():
        halo = jnp.where(r == 0, 0.0, uh_ref[0])
        ext = jnp.concatenate([halo, u_ref[0]], axis=0)
        pos = r * rows + lax.broadcasted_iota(jnp.int32, (rows, gd), 0)
        for g, win in enumerate(POOL_WINDOWS):
            sl = slice(g * gd, (g + 1) * gd)
            s = ext[:, sl]
            step = 1
            while step < win:
                s = s + pltpu.roll(s, step, axis=0)
                step *= 2
            cnt = jnp.minimum(pos + 1, win).astype(F32)
            pooled = s[HALO:] / cnt - ext[HALO:, sl]
            mixed = jnp.dot(pooled.astype(BF16), pw_ref[0, g], preferred_element_type=F32)
            gate = gp_ref[0, :, sl]
            gate = gate / (1.0 + jnp.exp(-gate))
            op_ref[:, sl] = (mixed * ps_ref[0, :, sl] * gate).astype(BF16)

    y_attn = jnp.dot(og_ref[0], wa_ref[0], preferred_element_type=F32)
    y_pool = jnp.dot(op_ref[...], wp_ref[0], preferred_element_type=F32)
    merged = _sigmoid(ma_ref[0]) * y_attn + _sigmoid(mp_ref[0]) * y_pool
    o_ref[0] = merged.astype(o_ref.dtype)


def _merge(zin, og, pool_w, pool_scale, w_attn_up, w_pool_up, layer):
    b, l, _ = zin.shape
    pw = og.shape[2]
    d = w_attn_up.shape[2]
    half = d // 2
    halo_blocks = ROW_TILE // HALO
    assert pw == half
    u_blk, gp_blk, ma_blk, mp_blk = 4, 5, 6, 6 + d // half

    def rows(blk_fn):
        return pl.BlockSpec((1, ROW_TILE, half), blk_fn)

    return pl.pallas_call(
        _merge_kernel,
        grid=(b, l // ROW_TILE, d // half),
        in_specs=[
            rows(lambda i, r, n: (i, r, u_blk)),
            pl.BlockSpec((1, HALO, pw), lambda i, r, n: (i, jnp.maximum(r * halo_blocks - 1, 0), u_blk)),
            rows(lambda i, r, n: (i, r, gp_blk)),
            rows(lambda i, r, n: (i, r, 0)),
            rows(lambda i, r, n: (i, r, ma_blk + n)),
            rows(lambda i, r, n: (i, r, mp_blk + n)),
            pl.BlockSpec((1,) + pool_w.shape[1:], lambda i, r, n: (layer, 0, 0, 0)),
            pl.BlockSpec((1, 1, pw), lambda i, r, n: (layer, 0, 0)),
            pl.BlockSpec((1, pw, half), lambda i, r, n: (layer, 0, n)),
            pl.BlockSpec((1, pw, half), lambda i, r, n: (layer, 0, n)),
        ],
        out_specs=rows(lambda i, r, n: (i, r, n)),
        out_shape=jax.ShapeDtypeStruct((b, l, d), BF16),
        scratch_shapes=[pltpu.VMEM((ROW_TILE, pw), BF16)],
        compiler_params=_params(3),
        name="pool_merge",
    )(zin, zin, zin, og, zin, zin, pool_w, pool_scale, w_attn_up, w_pool_up)


def _outproj_kernel(m_ref, w_ref, x_ref, g_ref, *out_refs):
    x = x_ref[0] + jnp.dot(m_ref[0], w_ref[0], preferred_element_type=F32)
    normed = x * _rms_scale(x) * g_ref[0]
    if len(out_refs) == 2:
        out_refs[0][0] = x
    out_refs[-1][0] = normed.astype(out_refs[-1].dtype)


def _outproj(merged, w_out, hs, gains, layer, gain_index, last):
    b, l, d = hs.shape
    row_spec = pl.BlockSpec((1, ROW_TILE, d), lambda i, r: (i, r, 0))
    if last:
        out_shape = [jax.ShapeDtypeStruct((b, l, d), F32)]
    else:
        out_shape = [jax.ShapeDtypeStruct((b, l, d), F32), jax.ShapeDtypeStruct((b, l, d), BF16)]
    return pl.pallas_call(
        _outproj_kernel,
        grid=(b, l // ROW_TILE),
        in_specs=[row_spec,
                  pl.BlockSpec((1, d, d), lambda i, r: (layer, 0, 0)),
                  row_spec,
                  pl.BlockSpec((1, 1, d), lambda i, r: (gain_index, 0, 0))],
        out_specs=[row_spec] * len(out_shape),
        out_shape=out_shape,
        compiler_params=_params(2),
        name="outproj",
    )(merged, w_out, hs, gains)


def kernel(x, meta_tokens, norm_gain, w_in, pool_w, pool_scale, w_attn_up, w_pool_up, w_out, final_gain):
    b, seq, d = x.shape
    depth = norm_gain.shape[0]
    assert meta_tokens.shape[0] == N_META and (seq + N_META) % ROW_TILE == 0
    meta = jnp.broadcast_to(meta_tokens.astype(x.dtype)[None], (b, N_META, d))
    hs = jnp.concatenate([meta, x], axis=1)

    gains = jnp.concatenate([norm_gain, final_gain[None]], axis=0)[:, None, :]
    pool_scale3 = pool_scale[:, None, :]
    pool_wb = pool_w.astype(BF16)
    wab = w_attn_up.astype(BF16)
    wpb = w_pool_up.astype(BF16)
    wob = w_out.astype(BF16)

    h = _prenorm(hs, gains, 0)
    for layer in range(depth):
        last = layer == depth - 1
        zin = _inproj(h, w_in, layer)
        og = _attention(zin)
        merged = _merge(zin, og, pool_wb, pool_scale3, wab, wpb, layer)
        outs = _outproj(merged, wob, hs, gains, layer, layer + 1, last)
        if last:
            return outs[0][:, N_META:]
        hs, h = outs
```

```python
import functools

import jax
import jax.numpy as jnp
from jax import lax
from jax.experimental import pallas as pl
from jax.experimental.pallas import tpu as pltpu

N_META = 16
HEADS = 8
HEAD_DIM = 128
POOL_WINDOWS = (2, 4, 8, 16)
RMS_EPS = 1e-6
KEY_BLOCK = 128
WINDOW_BLOCKS = 3
EXIT_SUM = 88.0
ROW_TILE = 688
HALO = 16
IN_TILE = 512
VMEM_LIMIT = 56 * 1024 * 1024

BF16 = jnp.bfloat16
F32 = jnp.float32


def _params(n_axes):
    return pltpu.CompilerParams(
        dimension_semantics=("arbitrary",) * n_axes,
        vmem_limit_bytes=VMEM_LIMIT)


def _sigmoid(x):
    return 1.0 / (1.0 + jnp.exp(-x))


def _rms_scale(x):
    return lax.rsqrt(jnp.mean(x * x, axis=-1, keepdims=True) + RMS_EPS)


def _norm_kernel(x_ref, g_ref, o_ref):
    x = x_ref[0]
    o_ref[0] = (x * _rms_scale(x) * g_ref[0]).astype(o_ref.dtype)


def _prenorm(hs, gains, layer):
    b, l, d = hs.shape
    return pl.pallas_call(
        _norm_kernel,
        grid=(b, l // ROW_TILE),
        in_specs=[pl.BlockSpec((1, ROW_TILE, d), lambda i, r: (i, r, 0)),
                  pl.BlockSpec((1, 1, d), lambda i, r: (layer, 0, 0))],
        out_specs=pl.BlockSpec((1, ROW_TILE, d), lambda i, r: (i, r, 0)),
        out_shape=jax.ShapeDtypeStruct((b, l, d), BF16),
        compiler_params=_params(2),
        name="prenorm",
    )(hs, gains)


def _inproj_kernel(h_ref, w_ref, o_ref):
    w = w_ref[0].astype(BF16)
    o_ref[0] = jnp.dot(h_ref[0], w, preferred_element_type=F32)


def _inproj(h, w_in, layer):
    b, l, d = h.shape
    n = w_in.shape[2]
    return pl.pallas_call(
        _inproj_kernel,
        grid=(b, n // IN_TILE),
        in_specs=[pl.BlockSpec((1, l, d), lambda i, j: (i, 0, 0)),
                  pl.BlockSpec((1, d, IN_TILE), lambda i, j: (layer, 0, j))],
        out_specs=pl.BlockSpec((1, l, IN_TILE), lambda i, j: (i, 0, j)),
        out_shape=jax.ShapeDtypeStruct((b, l, n), F32),
        compiler_params=_params(2),
        name="inproj",
    )(h, w_in)


def _softplus(z):
    return jnp.maximum(z, 0.0) + jnp.log(1.0 + jnp.exp(-jnp.abs(z)))


def _split_bf16(x):
    hi = x.astype(BF16)
    lo = (x - hi.astype(F32)).astype(BF16)
    return jnp.concatenate([hi, lo], axis=1)


def _attn_kernel(q_ref, k_ref, v_ref, g_ref, o_ref, qb_ref, kb_ref, vb_ref, c_ref, acc_ref, *, seq):
    kblk = KEY_BLOCK
    padded = qb_ref.shape[0]
    n_tiles = padded // kblk
    scale = HEAD_DIM ** -0.5

    for src, dst in ((q_ref, qb_ref), (k_ref, kb_ref), (v_ref, vb_ref)):
        dst[:seq, :] = src[0].astype(BF16)
        if padded > seq:
            dst[seq:, :] = jnp.zeros((padded - seq, HEAD_DIM), BF16)

    wr = lax.broadcasted_iota(jnp.int32, (2 * kblk, 2 * kblk), 0) & (kblk - 1)
    wc = lax.broadcasted_iota(jnp.int32, (2 * kblk, 2 * kblk), 1)
    w = jnp.where((wc >= kblk) | (wr > wc), 1.0, 0.0).astype(BF16)

    row = lax.broadcasted_iota(jnp.int32, (kblk, kblk), 0)
    col = lax.broadcasted_iota(jnp.int32, (kblk, kblk), 1)
    causal = col < row

    def scores(q, kk):
        return lax.dot_general(q, kk, (((1,), (1,)), ((), ())), preferred_element_type=F32) * scale

    for i in range(n_tiles):
        q0 = i * kblk
        k_lo = max(0, q0 - (WINDOW_BLOCKS - 1) * kblk)
        nb = (q0 + kblk - k_lo) // kblk
        z = scores(qb_ref[q0:q0 + kblk, :], kb_ref[k_lo:q0 + kblk, :])
        sp = _softplus(z)
        zs = [z[:, b * kblk:(b + 1) * kblk] for b in range(nb)]
        sps = [sp[:, b * kblk:(b + 1) * kblk] for b in range(nb)]
        parts = [_split_bf16(sps[b]) for b in range(nb - 1)]
        parts.append(_split_bf16(jnp.where(causal, sps[-1], 0.0)))
        r = jnp.dot(jnp.concatenate(parts, axis=0), w, preferred_element_type=F32)
        later = jnp.zeros((kblk, kblk), F32)
        probs = [None] * nb
        for b in reversed(range(nb)):
            rb = r[b * kblk:(b + 1) * kblk]
            a = jnp.exp(zs[b] - sps[b] - (rb[:, :kblk] + later))
            probs[b] = (jnp.where(causal, a, 0.0) if b == nb - 1 else a).astype(BF16)
            later = later + rb[:, kblk:]
        acc_ref[q0:q0 + kblk, :] = jnp.dot(jnp.concatenate(probs, axis=1), vb_ref[k_lo:q0 + kblk, :],
                                           preferred_element_type=F32)
        c_ref[q0:q0 + kblk, :] = later

    if n_tiles > WINDOW_BLOCKS:
        first = WINDOW_BLOCKS * kblk

        def block(q, k0, c, acc):
            z = scores(q, kb_ref[pl.ds(k0, kblk), :])
            sp = _softplus(z)
            r = jnp.dot(_split_bf16(sp), w, preferred_element_type=F32)
            a = jnp.exp(z - sp - (r[:, :kblk] + c))
            acc = acc + jnp.dot(a.astype(BF16), vb_ref[pl.ds(k0, kblk), :], preferred_element_type=F32)
            return c + r[:, kblk:], acc

        @pl.when(jnp.min(c_ref[first:, :]) < EXIT_SUM)
        def _():
            def tile(i, _):
                q0 = pl.multiple_of(i * kblk, kblk)
                q = qb_ref[pl.ds(q0, kblk), :]
                c0 = c_ref[pl.ds(q0, kblk), :]

                def more(state):
                    return jnp.logical_and(state[0] >= 0, state[1] < EXIT_SUM)

                def step(state):
                    j, _, c, acc = state
                    c, acc = block(q, pl.multiple_of(j * kblk, kblk), c, acc)
                    return j - 1, jnp.min(c), c, acc

                state = (i - WINDOW_BLOCKS, jnp.min(c0), c0, acc_ref[pl.ds(q0, kblk), :])
                acc_ref[pl.ds(q0, kblk), :] = lax.while_loop(more, step, state)[3]
                return 0

            lax.fori_loop(WINDOW_BLOCKS, n_tiles, tile, 0)

    g = g_ref[0]
    o_ref[0] = (acc_ref[:seq, :] * (g * _sigmoid(g))).astype(o_ref.dtype)


def _attention(zin):
    b, l, _ = zin.shape
    width = HEADS * HEAD_DIM
    padded = pl.cdiv(l, KEY_BLOCK) * KEY_BLOCK
    assert l % 16 == 0

    def spec(part):
        return pl.BlockSpec((1, l, HEAD_DIM), lambda i, h: (i, 0, part * HEADS + h))

    return pl.pallas_call(
        functools.partial(_attn_kernel, seq=l),
        grid=(b, HEADS),
        in_specs=[spec(0), spec(1), spec(2), spec(3)],
        out_specs=pl.BlockSpec((1, l, HEAD_DIM), lambda i, h: (i, 0, h)),
        out_shape=jax.ShapeDtypeStruct((b, l, width), BF16),
        scratch_shapes=[pltpu.VMEM((padded, HEAD_DIM), BF16)] * 3
                       + [pltpu.VMEM((padded, HEAD_DIM), F32)] * 2,
        compiler_params=_params(2),
        name="stickbreak_attn",
    )(zin, zin, zin, zin)


def _merge_kernel(u_ref, uh_ref, gp_ref, og_ref, ma_ref, mp_ref, pw_ref, ps_ref, wa_ref, wp_ref,
                  o_ref, op_ref):
    r = pl.program_id(1)
    n = pl.program_id(2)
    rows = u_ref.shape[1]
    gd = pw_ref.shape[2]

    @pl.when(n == 0)
    def _():
        halo = jnp.where(r == 0, 0.0, uh_ref[0])
        ext = jnp.concatenate([halo, u_ref[0]], axis=0)
        pos = r * rows + lax.broadcasted_iota(jnp.int32, (rows, gd), 0)
        for g, win in enumerate(POOL_WINDOWS):
            sl = slice(g * gd, (g + 1) * gd)
            s = ext[:, sl]
            step = 1
            while step < win:
                s = s + pltpu.roll(s, step, axis=0)
                step *= 2
            cnt = jnp.minimum(pos + 1, win).astype(F32)
            pooled = s[HALO:] / cnt - ext[HALO:, sl]
            mixed = jnp.dot(pooled.astype(BF16), pw_ref[0, g], preferred_element_type=F32)
            gate = gp_ref[0, :, sl]
            gate = gate / (1.0 + jnp.exp(-gate))
            op_ref[:, sl] = (mixed * ps_ref[0, :, sl] * gate).astype(BF16)

    y_attn = jnp.dot(og_ref[0], wa_ref[0], preferred_element_type=F32)
    y_pool = jnp.dot(op_ref[...], wp_ref[0], preferred_element_type=F32)
    merged = _sigmoid(ma_ref[0]) * y_attn + _sigmoid(mp_ref[0]) * y_pool
    o_ref[0] = merged.astype(o_ref.dtype)


def _merge(zin, og, pool_w, pool_scale, w_attn_up, w_pool_up, layer):
    b, l, _ = zin.shape
    pw = og.shape[2]
    d = w_attn_up.shape[2]
    half = d // 2
    halo_blocks = ROW_TILE // HALO
    assert pw == half
    u_blk, gp_blk, ma_blk, mp_blk = 4, 5, 6, 6 + d // half

    def rows(blk_fn):
        return pl.BlockSpec((1, ROW_TILE, half), blk_fn)

    return pl.pallas_call(
        _merge_kernel,
        grid=(b, l // ROW_TILE, d // half),
        in_specs=[
            rows(lambda i, r, n: (i, r, u_blk)),
            pl.BlockSpec((1, HALO, pw), lambda i, r, n: (i, jnp.maximum(r * halo_blocks - 1, 0), u_blk)),
            rows(lambda i, r, n: (i, r, gp_blk)),
            rows(lambda i, r, n: (i, r, 0)),
            rows(lambda i, r, n: (i, r, ma_blk + n)),
            rows(lambda i, r, n: (i, r, mp_blk + n)),
            pl.BlockSpec((1,) + pool_w.shape[1:], lambda i, r, n: (layer, 0, 0, 0)),
            pl.BlockSpec((1, 1, pw), lambda i, r, n: (layer, 0, 0)),
            pl.BlockSpec((1, pw, half), lambda i, r, n: (layer, 0, n)),
            pl.BlockSpec((1, pw, half), lambda i, r, n: (layer, 0, n)),
        ],
        out_specs=rows(lambda i, r, n: (i, r, n)),
        out_shape=jax.ShapeDtypeStruct((b, l, d), BF16),
        scratch_shapes=[pltpu.VMEM((ROW_TILE, pw), BF16)],
        compiler_params=_params(3),
        name="pool_merge",
    )(zin, zin, zin, og, zin, zin, pool_w, pool_scale, w_attn_up, w_pool_up)


def _outproj_kernel(m_ref, w_ref, x_ref, g_ref, *out_refs):
    x = x_ref[0] + jnp.dot(m_ref[0], w_ref[0], preferred_element_type=F32)
    normed = x * _rms_scale(x) * g_ref[0]
    if len(out_refs) == 2:
        out_refs[0][0] = x
    out_refs[-1][0] = normed.astype(out_refs[-1].dtype)


def _outproj(merged, w_out, hs, gains, layer, gain_index, last):
    b, l, d = hs.shape
    row_spec = pl.BlockSpec((1, ROW_TILE, d), lambda i, r: (i, r, 0))
    if last:
        out_shape = [jax.ShapeDtypeStruct((b, l, d), F32)]
    else:
        out_shape = [jax.ShapeDtypeStruct((b, l, d), F32), jax.ShapeDtypeStruct((b, l, d), BF16)]
    return pl.pallas_call(
        _outproj_kernel,
        grid=(b, l // ROW_TILE),
        in_specs=[row_spec,
                  pl.BlockSpec((1, d, d), lambda i, r: (layer, 0, 0)),
                  row_spec,
                  pl.BlockSpec((1, 1, d), lambda i, r: (gain_index, 0, 0))],
        out_specs=[row_spec] * len(out_shape),
        out_shape=out_shape,
        compiler_params=_params(2),
        name="outproj",
    )(merged, w_out, hs, gains)


def kernel(x, meta_tokens, norm_gain, w_in, pool_w, pool_scale, w_attn_up, w_pool_up, w_out, final_gain):
    b, seq, d = x.shape
    depth = norm_gain.shape[0]
    assert meta_tokens.shape[0] == N_META and (seq + N_META) % ROW_TILE == 0
    meta = jnp.broadcast_to(meta_tokens.astype(x.dtype)[None], (b, N_META, d))
    hs = jnp.concatenate([meta, x], axis=1)

    gains = jnp.concatenate([norm_gain, final_gain[None]], axis=0)[:, None, :]
    pool_scale3 = pool_scale[:, None, :]
    pool_wb = pool_w.astype(BF16)
    wab = w_attn_up.astype(BF16)
    wpb = w_pool_up.astype(BF16)
    wob = w_out.astype(BF16)

    h = _prenorm(hs, gains, 0)
    for layer in range(depth):
        last = layer == depth - 1
        zin = _inproj(h, w_in, layer)
        og = _attention(zin)
        merged = _merge(zin, og, pool_wb, pool_scale3, wab, wpb, layer)
        outs = _outproj(merged, wob, hs, gains, layer, layer + 1, last)
        if last:
            return outs[0][:, N_META:]
        hs, h = outs
```

```python
import functools

import jax
import jax.numpy as jnp
from jax import lax
from jax.experimental import pallas as pl
from jax.experimental.pallas import tpu as pltpu

N_META = 16
HEADS = 8
HEAD_DIM = 128
POOL_WINDOWS = (2, 4, 8, 16)
RMS_EPS = 1e-6
LOG2_E = 1.4426950408889634
SOFTPLUS2_CLAMP = 126.0
KEY_BLOCK = 128
WINDOW_BLOCKS = 3
EXIT_SUM = 127.0
HEAD_GROUP = 2
STAGE_SKEW = 2
ROW_TILE = 688
HALO = 16
IN_TILE = 512
VMEM_LIMIT = 56 * 1024 * 1024

BF16 = jnp.bfloat16
F32 = jnp.float32


def _params(n_axes):
    return pltpu.CompilerParams(
        dimension_semantics=("arbitrary",) * n_axes,
        vmem_limit_bytes=VMEM_LIMIT)


def _sigmoid(x):
    return 1.0 / (1.0 + jnp.exp(-x))


def _rms_scale(x):
    return lax.rsqrt(jnp.mean(x * x, axis=-1, keepdims=True) + RMS_EPS)


def _norm_kernel(x_ref, g_ref, o_ref):
    x = x_ref[0]
    o_ref[0] = (x * _rms_scale(x) * g_ref[0]).astype(o_ref.dtype)


def _prenorm(hs, gains, layer):
    b, l, d = hs.shape
    return pl.pallas_call(
        _norm_kernel,
        grid=(b, l // ROW_TILE),
        in_specs=[pl.BlockSpec((1, ROW_TILE, d), lambda i, r: (i, r, 0)),
                  pl.BlockSpec((1, 1, d), lambda i, r: (layer, 0, 0))],
        out_specs=pl.BlockSpec((1, ROW_TILE, d), lambda i, r: (i, r, 0)),
        out_shape=jax.ShapeDtypeStruct((b, l, d), BF16),
        compiler_params=_params(2),
        name="prenorm",
    )(hs, gains)


def _inproj_kernel(h_ref, w_ref, o_ref):
    w = w_ref[0].astype(BF16)
    o_ref[0] = jnp.dot(h_ref[0], w, preferred_element_type=F32)


def _inproj(h, w_in, layer):
    b, l, d = h.shape
    n = w_in.shape[2]
    return pl.pallas_call(
        _inproj_kernel,
        grid=(b, n // IN_TILE),
        in_specs=[pl.BlockSpec((1, l, d), lambda i, j: (i, 0, 0)),
                  pl.BlockSpec((1, d, IN_TILE), lambda i, j: (layer, 0, j))],
        out_specs=pl.BlockSpec((1, l, IN_TILE), lambda i, j: (i, 0, j)),
        out_shape=jax.ShapeDtypeStruct((b, l, n), F32),
        compiler_params=_params(2),
        name="inproj",
    )(h, w_in)


def _softplus2(y):
    return jnp.maximum(y, jnp.log2(1.0 + jnp.exp2(jnp.minimum(y, SOFTPLUS2_CLAMP))))


def _split_bf16(x):
    hi = x.astype(BF16)
    lo = (x - hi.astype(F32)).astype(BF16)
    return jnp.concatenate([hi, lo], axis=1)


def _attn_kernel(q_ref, k_ref, v_ref, g_ref, o_ref, qb_ref, kb_ref, vb_ref, c_ref, acc_ref, *, seq):
    kblk = KEY_BLOCK
    padded, lanes = qb_ref.shape
    n_tiles = padded // kblk
    heads = [slice(h * HEAD_DIM, (h + 1) * HEAD_DIM) for h in range(lanes // HEAD_DIM)]

    qb_ref[:seq, :] = (q_ref[0] * (HEAD_DIM ** -0.5 * LOG2_E)).astype(BF16)
    kb_ref[:seq, :] = k_ref[0].astype(BF16)
    vb_ref[:seq, :] = v_ref[0].astype(BF16)
    if padded > seq:
        for dst in (qb_ref, kb_ref, vb_ref):
            dst[seq:, :] = jnp.zeros((padded - seq, lanes), BF16)

    wr = lax.broadcasted_iota(jnp.int32, (2 * kblk, 2 * kblk), 0) & (kblk - 1)
    wc = lax.broadcasted_iota(jnp.int32, (2 * kblk, 2 * kblk), 1)
    w = jnp.where((wc >= kblk) | (wr >= wc), 1.0, 0.0).astype(BF16)

    row = lax.broadcasted_iota(jnp.int32, (kblk, kblk), 0)
    col = lax.broadcasted_iota(jnp.int32, (kblk, kblk), 1)
    causal = col < row

    def scores(q, kk):
        return lax.dot_general(q, kk, (((1,), (1,)), ((), ())), preferred_element_type=F32)

    def window(i):
        q0 = i * kblk
        k_lo = max(0, q0 - (WINDOW_BLOCKS - 1) * kblk)
        return q0, k_lo, (q0 + kblk - k_lo) // kblk

    def stage_scores(i, hd):
        q0, k_lo, _ = window(i)
        return scores(qb_ref[q0:q0 + kblk, hd], kb_ref[k_lo:q0 + kblk, hd])

    def stage_sums(i, hd, y):
        nb = window(i)[2]
        sp = _softplus2(y)
        parts = [_split_bf16(sp[:, b * kblk:(b + 1) * kblk]) for b in range(nb - 1)]
        parts.append(_split_bf16(jnp.where(causal, sp[:, (nb - 1) * kblk:], 0.0)))
        return y, jnp.dot(jnp.concatenate(parts, axis=0), w, preferred_element_type=F32)

    def stage_values(i, hd, y, r):
        q0, k_lo, nb = window(i)
        later = jnp.zeros((kblk, kblk), F32)
        probs = [None] * nb
        for b in reversed(range(nb)):
            rb = r[b * kblk:(b + 1) * kblk]
            a = jnp.exp2(y[:, b * kblk:(b + 1) * kblk] - rb[:, :kblk] - later)
            probs[b] = (jnp.where(causal, a, 0.0) if b == nb - 1 else a).astype(BF16)
            later = later + rb[:, kblk:]
        acc_ref[q0:q0 + kblk, hd] = jnp.dot(jnp.concatenate(probs, axis=1),
                                            vb_ref[k_lo:q0 + kblk, hd], preferred_element_type=F32)
        c_ref[q0:q0 + kblk, hd] = later

    units = [(i, hd) for i in range(n_tiles) for hd in heads]
    ys, sums = {}, {}
    for s in range(len(units) + 2 * STAGE_SKEW):
        if s < len(units):
            ys[s] = stage_scores(*units[s])
        if 0 <= s - STAGE_SKEW < len(units):
            sums[s - STAGE_SKEW] = stage_sums(*units[s - STAGE_SKEW], ys.pop(s - STAGE_SKEW))
        if 0 <= s - 2 * STAGE_SKEW < len(units):
            stage_values(*units[s - 2 * STAGE_SKEW], *sums.pop(s - 2 * STAGE_SKEW))

    if n_tiles > WINDOW_BLOCKS:
        first = WINDOW_BLOCKS * kblk

        def block(hd, q, k0, c, acc):
            y = scores(q, kb_ref[pl.ds(k0, kblk), hd])
            sp = _softplus2(y)
            r = jnp.dot(_split_bf16(sp), w, preferred_element_type=F32)
            a = jnp.exp2(y - r[:, :kblk] - c)
            acc = acc + jnp.dot(a.astype(BF16), vb_ref[pl.ds(k0, kblk), hd], preferred_element_type=F32)
            return c + r[:, kblk:], acc

        @pl.when(jnp.min(c_ref[first:, :]) < EXIT_SUM)
        def _():
            def tile(i, _):
                q0 = pl.multiple_of(i * kblk, kblk)
                for hd in heads:
                    q = qb_ref[pl.ds(q0, kblk), hd]
                    c0 = c_ref[pl.ds(q0, kblk), hd]

                    def more(state):
                        return jnp.logical_and(state[0] >= 0, state[1] < EXIT_SUM)

                    def step(state, hd=hd, q=q):
                        j, _, c, acc = state
                        c, acc = block(hd, q, pl.multiple_of(j * kblk, kblk), c, acc)
                        return j - 1, jnp.min(c), c, acc

                    state = (i - WINDOW_BLOCKS, jnp.min(c0), c0, acc_ref[pl.ds(q0, kblk), hd])
                    acc_ref[pl.ds(q0, kblk), hd] = lax.while_loop(more, step, state)[3]
                return 0

            lax.fori_loop(WINDOW_BLOCKS, n_tiles, tile, 0)

    g = g_ref[0]
    o_ref[0] = (acc_ref[:seq, :] * (g * _sigmoid(g))).astype(o_ref.dtype)


def _attention(zin):
    b, l, _ = zin.shape
    width = HEADS * HEAD_DIM
    lanes = HEAD_GROUP * HEAD_DIM
    groups = HEADS // HEAD_GROUP
    padded = pl.cdiv(l, KEY_BLOCK) * KEY_BLOCK
    assert l % 16 == 0

    def spec(part):
        return pl.BlockSpec((1, l, lanes), lambda i, h: (i, 0, part * groups + h))

    return pl.pallas_call(
        functools.partial(_attn_kernel, seq=l),
        grid=(b, groups),
        in_specs=[spec(0), spec(1), spec(2), spec(3)],
        out_specs=pl.BlockSpec((1, l, lanes), lambda i, h: (i, 0, h)),
        out_shape=jax.ShapeDtypeStruct((b, l, width), BF16),
        scratch_shapes=[pltpu.VMEM((padded, lanes), BF16)] * 3 + [pltpu.VMEM((padded, lanes), F32)] * 2,
        compiler_params=_params(2),
        name="stickbreak_attn",
    )(zin, zin, zin, zin)


def _merge_kernel(u_ref, uh_ref, gp_ref, og_ref, ma_ref, mp_ref, pw_ref, ps_ref, wa_ref, wp_ref,
                  o_ref, op_ref):
    r = pl.program_id(1)
    n = pl.program_id(2)
    rows = u_ref.shape[1]
    gd = pw_ref.shape[2]

    @pl.when(n == 0)
    def _():
        halo = jnp.where(r == 0, 0.0, uh_ref[0])
        ext = jnp.concatenate([halo, u_ref[0]], axis=0)
        pos = r * rows + lax.broadcasted_iota(jnp.int32, (rows, gd), 0)
        for g, win in enumerate(POOL_WINDOWS):
            sl = slice(g * gd, (g + 1) * gd)
            s = ext[:, sl]
            step = 1
            while step < win:
                s = s + pltpu.roll(s, step, axis=0)
                step *= 2
            cnt = jnp.minimum(pos + 1, win).astype(F32)
            pooled = s[HALO:] / cnt - ext[HALO:, sl]
            mixed = jnp.dot(pooled.astype(BF16), pw_ref[0, g], preferred_element_type=F32)
            gate = gp_ref[0, :, sl]
            gate = gate / (1.0 + jnp.exp(-gate))
            op_ref[:, sl] = (mixed * ps_ref[0, :, sl] * gate).astype(BF16)

    half = o_ref.shape[2]
    cols = pl.ds(pl.multiple_of(n * half, half), half)
    y_attn = jnp.dot(og_ref[0], wa_ref[0, :, cols], preferred_element_type=F32)
    y_pool = jnp.dot(op_ref[...], wp_ref[0, :, cols], preferred_element_type=F32)
    merged = _sigmoid(ma_ref[0]) * y_attn + _sigmoid(mp_ref[0]) * y_pool
    o_ref[0] = merged.astype(o_ref.dtype)


def _merge(zin, og, pool_w, pool_scale, w_attn_up, w_pool_up, layer):
    b, l, _ = zin.shape
    pw = og.shape[2]
    d = w_attn_up.shape[2]
    half = d // 2
    halo_blocks = ROW_TILE // HALO
    assert pw == half
    u_blk, gp_blk, ma_blk, mp_blk = 4, 5, 6, 6 + d // half

    def rows(blk_fn):
        return pl.BlockSpec((1, ROW_TILE, half), blk_fn)

    return pl.pallas_call(
        _merge_kernel,
        grid=(b, l // ROW_TILE, d // half),
        in_specs=[
            rows(lambda i, r, n: (i, r, u_blk)),
            pl.BlockSpec((1, HALO, pw), lambda i, r, n: (i, jnp.maximum(r * halo_blocks - 1, 0), u_blk)),
            rows(lambda i, r, n: (i, r, gp_blk)),
            rows(lambda i, r, n: (i, r, 0)),
            rows(lambda i, r, n: (i, r, ma_blk + n)),
            rows(lambda i, r, n: (i, r, mp_blk + n)),
            pl.BlockSpec((1,) + pool_w.shape[1:], lambda i, r, n: (layer, 0, 0, 0)),
            pl.BlockSpec((1, 1, pw), lambda i, r, n: (layer, 0, 0)),
            pl.BlockSpec((1, pw, d), lambda i, r, n: (layer, 0, 0)),
            pl.BlockSpec((1, pw, d), lambda i, r, n: (layer, 0, 0)),
        ],
        out_specs=rows(lambda i, r, n: (i, r, n)),
        out_shape=jax.ShapeDtypeStruct((b, l, d), BF16),
        scratch_shapes=[pltpu.VMEM((ROW_TILE, pw), BF16)],
        compiler_params=_params(3),
        name="pool_merge",
    )(zin, zin, zin, og, zin, zin, pool_w, pool_scale, w_attn_up, w_pool_up)


def _outproj_kernel(m_ref, w_ref, x_ref, g_ref, *out_refs):
    x = x_ref[0] + jnp.dot(m_ref[0], w_ref[0], preferred_element_type=F32)
    normed = x * _rms_scale(x) * g_ref[0]
    if len(out_refs) == 2:
        out_refs[0][0] = x
    out_refs[-1][0] = normed.astype(out_refs[-1].dtype)


def _outproj(merged, w_out, hs, gains, layer, gain_index, last):
    b, l, d = hs.shape
    row_spec = pl.BlockSpec((1, ROW_TILE, d), lambda i, r: (i, r, 0))
    if last:
        out_shape = [jax.ShapeDtypeStruct((b, l, d), F32)]
    else:
        out_shape = [jax.ShapeDtypeStruct((b, l, d), F32), jax.ShapeDtypeStruct((b, l, d), BF16)]
    return pl.pallas_call(
        _outproj_kernel,
        grid=(b, l // ROW_TILE),
        in_specs=[row_spec,
                  pl.BlockSpec((1, d, d), lambda i, r: (layer, 0, 0)),
                  row_spec,
                  pl.BlockSpec((1, 1, d), lambda i, r: (gain_index, 0, 0))],
        out_specs=[row_spec] * len(out_shape),
        out_shape=out_shape,
        compiler_params=_params(2),
        name="outproj",
    )(merged, w_out, hs, gains)


def kernel(x, meta_tokens, norm_gain, w_in, pool_w, pool_scale, w_attn_up, w_pool_up, w_out, final_gain):
    b, seq, d = x.shape
    depth = norm_gain.shape[0]
    assert meta_tokens.shape[0] == N_META and (seq + N_META) % ROW_TILE == 0
    meta = jnp.broadcast_to(meta_tokens.astype(x.dtype)[None], (b, N_META, d))
    hs = jnp.concatenate([meta, x], axis=1)

    gains = jnp.concatenate([norm_gain, final_gain[None]], axis=0)[:, None, :]
    pool_scale3 = pool_scale[:, None, :]
    pool_wb = pool_w.astype(BF16)
    wab = w_attn_up.astype(BF16)
    wpb = w_pool_up.astype(BF16)
    wob = w_out.astype(BF16)

    h = _prenorm(hs, gains, 0)
    for layer in range(depth):
        last = layer == depth - 1
        zin = _inproj(h, w_in, layer)
        og = _attention(zin)
        merged = _merge(zin, og, pool_wb, pool_scale3, wab, wpb, layer)
        outs = _outproj(merged, wob, hs, gains, layer, layer + 1, last)
        if last:
            return outs[0][:, N_META:]
        hs, h = outs
```

```python
import functools

import jax
import jax.numpy as jnp
from jax import lax
from jax.experimental import pallas as pl
from jax.experimental.pallas import tpu as pltpu

N_META = 16
HEADS = 8
HEAD_DIM = 128
POOL_WINDOWS = (2, 4, 8, 16)
RMS_EPS = 1e-6
LOG2_E = 1.4426950408889634
SOFTPLUS2_CLAMP = 126.0
KEY_BLOCK = 128
WINDOW_BLOCKS = 3
EXIT_SUM = 127.0
HEAD_GROUP = 2
STAGE_SKEW = 2
ROW_TILE = 688
MIX_TILE = 344
MERGE_SPLIT = 2
HALO = 16
IN_TILE = 512
VMEM_LIMIT = 56 * 1024 * 1024

BF16 = jnp.bfloat16
F32 = jnp.float32


def _params(n_axes):
    return pltpu.CompilerParams(
        dimension_semantics=("arbitrary",) * n_axes,
        vmem_limit_bytes=VMEM_LIMIT)


def _sigmoid(x):
    return 1.0 / (1.0 + jnp.exp(-x))


def _rms_scale(x):
    return lax.rsqrt(jnp.mean(x * x, axis=-1, keepdims=True) + RMS_EPS)


def _norm_kernel(x_ref, g_ref, o_ref):
    x = x_ref[0]
    o_ref[0] = (x * _rms_scale(x) * g_ref[0]).astype(o_ref.dtype)


def _prenorm(hs, gains, layer):
    b, l, d = hs.shape
    return pl.pallas_call(
        _norm_kernel,
        grid=(b, l // ROW_TILE),
        in_specs=[pl.BlockSpec((1, ROW_TILE, d), lambda i, r: (i, r, 0)),
                  pl.BlockSpec((1, 1, d), lambda i, r: (layer, 0, 0))],
        out_specs=pl.BlockSpec((1, ROW_TILE, d), lambda i, r: (i, r, 0)),
        out_shape=jax.ShapeDtypeStruct((b, l, d), BF16),
        compiler_params=_params(2),
        name="prenorm",
    )(hs, gains)


def _inproj_kernel(h_ref, w_ref, o_ref):
    w = w_ref[0].astype(BF16)
    o_ref[0] = jnp.dot(h_ref[0], w, preferred_element_type=F32)


def _inproj(h, w_in, layer):
    b, l, d = h.shape
    n = w_in.shape[2]
    return pl.pallas_call(
        _inproj_kernel,
        grid=(b, n // IN_TILE),
        in_specs=[pl.BlockSpec((1, l, d), lambda i, j: (i, 0, 0)),
                  pl.BlockSpec((1, d, IN_TILE), lambda i, j: (layer, 0, j))],
        out_specs=pl.BlockSpec((1, l, IN_TILE), lambda i, j: (i, 0, j)),
        out_shape=jax.ShapeDtypeStruct((b, l, n), F32),
        compiler_params=_params(2),
        name="inproj",
    )(h, w_in)


def _softplus2(y):
    return jnp.maximum(y, jnp.log2(1.0 + jnp.exp2(jnp.minimum(y, SOFTPLUS2_CLAMP))))


def _split_bf16(x):
    hi = x.astype(BF16)
    lo = (x - hi.astype(F32)).astype(BF16)
    return jnp.concatenate([hi, lo], axis=1)


def _attn_kernel(q_ref, k_ref, v_ref, g_ref, o_ref, qb_ref, kb_ref, vb_ref, c_ref, acc_ref, *, seq):
    kblk = KEY_BLOCK
    padded, lanes = qb_ref.shape
    n_tiles = padded // kblk
    heads = [slice(h * HEAD_DIM, (h + 1) * HEAD_DIM) for h in range(lanes // HEAD_DIM)]

    qb_ref[:seq, :] = (q_ref[0] * (HEAD_DIM ** -0.5 * LOG2_E)).astype(BF16)
    kb_ref[:seq, :] = k_ref[0].astype(BF16)
    vb_ref[:seq, :] = v_ref[0].astype(BF16)
    if padded > seq:
        for dst in (qb_ref, kb_ref, vb_ref):
            dst[seq:, :] = jnp.zeros((padded - seq, lanes), BF16)

    wr = lax.broadcasted_iota(jnp.int32, (2 * kblk, 2 * kblk), 0) & (kblk - 1)
    wc = lax.broadcasted_iota(jnp.int32, (2 * kblk, 2 * kblk), 1)
    w = jnp.where((wc >= kblk) | (wr >= wc), 1.0, 0.0).astype(BF16)

    row = lax.broadcasted_iota(jnp.int32, (kblk, kblk), 0)
    col = lax.broadcasted_iota(jnp.int32, (kblk, kblk), 1)
    causal = col < row

    def scores(q, kk):
        return lax.dot_general(q, kk, (((1,), (1,)), ((), ())), preferred_element_type=F32)

    def window(i):
        q0 = i * kblk
        k_lo = max(0, q0 - (WINDOW_BLOCKS - 1) * kblk)
        return q0, k_lo, (q0 + kblk - k_lo) // kblk

    def stage_scores(i, hd):
        q0, k_lo, _ = window(i)
        return scores(qb_ref[q0:q0 + kblk, hd], kb_ref[k_lo:q0 + kblk, hd])

    def stage_sums(i, hd, y):
        nb = window(i)[2]
        sp = _softplus2(y)
        parts = [_split_bf16(sp[:, b * kblk:(b + 1) * kblk]) for b in range(nb - 1)]
        parts.append(_split_bf16(jnp.where(causal, sp[:, (nb - 1) * kblk:], 0.0)))
        return y, jnp.dot(jnp.concatenate(parts, axis=0), w, preferred_element_type=F32)

    def stage_values(i, hd, y, r):
        q0, k_lo, nb = window(i)
        later = jnp.zeros((kblk, kblk), F32)
        probs = [None] * nb
        for b in reversed(range(nb)):
            rb = r[b * kblk:(b + 1) * kblk]
            a = jnp.exp2(y[:, b * kblk:(b + 1) * kblk] - rb[:, :kblk] - later)
            probs[b] = (jnp.where(causal, a, 0.0) if b == nb - 1 else a).astype(BF16)
            later = later + rb[:, kblk:]
        acc_ref[q0:q0 + kblk, hd] = jnp.dot(jnp.concatenate(probs, axis=1),
                                            vb_ref[k_lo:q0 + kblk, hd], preferred_element_type=F32)
        c_ref[q0:q0 + kblk, hd] = later

    units = [(i, hd) for i in range(n_tiles) for hd in heads]
    ys, sums = {}, {}
    for s in range(len(units) + 2 * STAGE_SKEW):
        if s < len(units):
            ys[s] = stage_scores(*units[s])
        if 0 <= s - STAGE_SKEW < len(units):
            sums[s - STAGE_SKEW] = stage_sums(*units[s - STAGE_SKEW], ys.pop(s - STAGE_SKEW))
        if 0 <= s - 2 * STAGE_SKEW < len(units):
            stage_values(*units[s - 2 * STAGE_SKEW], *sums.pop(s - 2 * STAGE_SKEW))

    if n_tiles > WINDOW_BLOCKS:
        first = WINDOW_BLOCKS * kblk

        def block(hd, q, k0, c, acc):
            y = scores(q, kb_ref[pl.ds(k0, kblk), hd])
            sp = _softplus2(y)
            r = jnp.dot(_split_bf16(sp), w, preferred_element_type=F32)
            a = jnp.exp2(y - r[:, :kblk] - c)
            acc = acc + jnp.dot(a.astype(BF16), vb_ref[pl.ds(k0, kblk), hd], preferred_element_type=F32)
            return c + r[:, kblk:], acc

        @pl.when(jnp.min(c_ref[first:, :]) < EXIT_SUM)
        def _():
            def tile(i, _):
                q0 = pl.multiple_of(i * kblk, kblk)
                for hd in heads:
                    q = qb_ref[pl.ds(q0, kblk), hd]
                    c0 = c_ref[pl.ds(q0, kblk), hd]

                    def more(state):
                        return jnp.logical_and(state[0] >= 0, state[1] < EXIT_SUM)

                    def step(state, hd=hd, q=q):
                        j, _, c, acc = state
                        c, acc = block(hd, q, pl.multiple_of(j * kblk, kblk), c, acc)
                        return j - 1, jnp.min(c), c, acc

                    state = (i - WINDOW_BLOCKS, jnp.min(c0), c0, acc_ref[pl.ds(q0, kblk), hd])
                    acc_ref[pl.ds(q0, kblk), hd] = lax.while_loop(more, step, state)[3]
                return 0

            lax.fori_loop(WINDOW_BLOCKS, n_tiles, tile, 0)

    g = g_ref[0]
    o_ref[0] = (acc_ref[:seq, :] * (g * _sigmoid(g))).astype(o_ref.dtype)


def _attention(zin):
    b, l, _ = zin.shape
    width = HEADS * HEAD_DIM
    lanes = HEAD_GROUP * HEAD_DIM
    groups = HEADS // HEAD_GROUP
    padded = pl.cdiv(l, KEY_BLOCK) * KEY_BLOCK
    assert l % 16 == 0

    def spec(part):
        return pl.BlockSpec((1, l, lanes), lambda i, h: (i, 0, part * groups + h))

    return pl.pallas_call(
        functools.partial(_attn_kernel, seq=l),
        grid=(b, groups),
        in_specs=[spec(0), spec(1), spec(2), spec(3)],
        out_specs=pl.BlockSpec((1, l, lanes), lambda i, h: (i, 0, h)),
        out_shape=jax.ShapeDtypeStruct((b, l, width), BF16),
        scratch_shapes=[pltpu.VMEM((padded, lanes), BF16)] * 3 + [pltpu.VMEM((padded, lanes), F32)] * 2,
        compiler_params=_params(2),
        name="stickbreak_attn",
    )(zin, zin, zin, zin)


def _mix_kernel(u_ref, uh0_ref, uh1_ref, gp_ref, og_ref, ma_ref, mp_ref, x_ref, pw_ref, ps_ref,
                wa_ref, wp_ref, wo_ref, g_ref, *rest):
    out_refs, (op_ref, mg_ref) = rest[:-2], rest[-2:]
    r = pl.program_id(1)
    rows = u_ref.shape[1]
    gd = pw_ref.shape[2]

    halo = jnp.concatenate([uh0_ref[0], uh1_ref[0]], axis=0)
    ext = jnp.concatenate([jnp.where(r == 0, 0.0, halo), u_ref[0]], axis=0)
    pos = r * rows + lax.broadcasted_iota(jnp.int32, (rows, gd), 0)
    for g, win in enumerate(POOL_WINDOWS):
        sl = slice(g * gd, (g + 1) * gd)
        s = ext[:, sl]
        step = 1
        while step < win:
            s = s + pltpu.roll(s, step, axis=0)
            step *= 2
        cnt = jnp.minimum(pos + 1, win).astype(F32)
        pooled = s[HALO:] / cnt - ext[HALO:, sl]
        mixed = jnp.dot(pooled.astype(BF16), pw_ref[0, g], preferred_element_type=F32)
        gate = gp_ref[0, :, sl]
        op_ref[:, sl] = (mixed * ps_ref[0, :, sl] * (gate * _sigmoid(gate))).astype(BF16)

    d = mg_ref.shape[1]
    for n in range(MERGE_SPLIT):
        cols = slice(n * d // MERGE_SPLIT, (n + 1) * d // MERGE_SPLIT)
        y_attn = jnp.dot(og_ref[0], wa_ref[0, :, cols], preferred_element_type=F32)
        y_pool = jnp.dot(op_ref[...], wp_ref[0, :, cols], preferred_element_type=F32)
        merged = _sigmoid(ma_ref[0, :, cols]) * y_attn + _sigmoid(mp_ref[0, :, cols]) * y_pool
        mg_ref[:, cols] = merged.astype(BF16)

    x = x_ref[0] + jnp.dot(mg_ref[...], wo_ref[0], preferred_element_type=F32)
    normed = x * _rms_scale(x) * g_ref[0]
    if len(out_refs) == 2:
        out_refs[0][0] = x
    out_refs[-1][0] = normed.astype(out_refs[-1].dtype)


def _mix(zin, og, hs, pool_w, pool_scale, w_attn_up, w_pool_up, w_out, gains, layer, last):
    b, l, d = hs.shape
    pw = og.shape[2]
    half_halo = HALO // 2
    halo_blocks = MIX_TILE // half_halo
    u_blk, gp_blk = 4, 5
    ma_blk, mp_blk = 6 * pw // d, 6 * pw // d + 1

    def rows(width, blk):
        return pl.BlockSpec((1, MIX_TILE, width), lambda i, r: (i, r, blk))

    def halo(k):
        return pl.BlockSpec((1, half_halo, pw),
                            lambda i, r: (i, jnp.maximum(r * halo_blocks - 2 + k, 0), u_blk))

    def resident(arr):
        return pl.BlockSpec((1,) + arr.shape[1:], lambda i, r: (layer,) + (0,) * (arr.ndim - 1))

    out_shape = [jax.ShapeDtypeStruct((b, l, d), F32)]
    if not last:
        out_shape.append(jax.ShapeDtypeStruct((b, l, d), BF16))
    return pl.pallas_call(
        _mix_kernel,
        grid=(b, l // MIX_TILE),
        in_specs=[rows(pw, u_blk), halo(0), halo(1), rows(pw, gp_blk), rows(pw, 0),
                  rows(d, ma_blk), rows(d, mp_blk), rows(d, 0),
                  resident(pool_w), resident(pool_scale), resident(w_attn_up), resident(w_pool_up),
                  resident(w_out),
                  pl.BlockSpec((1, 1, d), lambda i, r: (layer + 1, 0, 0))],
        out_specs=[rows(d, 0)] * len(out_shape),
        out_shape=out_shape,
        scratch_shapes=[pltpu.VMEM((MIX_TILE, pw), BF16), pltpu.VMEM((MIX_TILE, d), BF16)],
        compiler_params=_params(2),
        name="mix_out",
    )(zin, zin, zin, zin, og, zin, zin, hs, pool_w, pool_scale, w_attn_up, w_pool_up, w_out, gains)


def kernel(x, meta_tokens, norm_gain, w_in, pool_w, pool_scale, w_attn_up, w_pool_up, w_out, final_gain):
    b, seq, d = x.shape
    depth = norm_gain.shape[0]
    assert meta_tokens.shape[0] == N_META and (seq + N_META) % ROW_TILE == 0
    meta = jnp.broadcast_to(meta_tokens.astype(x.dtype)[None], (b, N_META, d))
    hs = jnp.concatenate([meta, x], axis=1)

    gains = jnp.concatenate([norm_gain, final_gain[None]], axis=0)[:, None, :]
    pool_scale3 = pool_scale[:, None, :]
    pool_wb = pool_w.astype(BF16)
    wab = w_attn_up.astype(BF16)
    wpb = w_pool_up.astype(BF16)
    wob = w_out.astype(BF16)

    h = _prenorm(hs, gains, 0)
    for layer in range(depth):
        last = layer == depth - 1
        zin = _inproj(h, w_in, layer)
        og = _attention(zin)
        outs = _mix(zin, og, hs, pool_wb, pool_scale3, wab, wpb, wob, gains, layer, last)
        if last:
            return outs[0][:, N_META:]
        hs, h = outs
```

```python
import functools

import jax
import jax.numpy as jnp
from jax import lax
from jax.experimental import pallas as pl
from jax.experimental.pallas import tpu as pltpu

N_META = 16
HEADS = 8
HEAD_DIM = 128
POOL_WINDOWS = (2, 4, 8, 16)
RMS_EPS = 1e-6
LOG2_E = 1.4426950408889634
SOFTPLUS2_CLAMP = 126.0
KEY_BLOCK = 128
WINDOW_BLOCKS = 3
EXIT_SUM = 127.0
HEAD_GROUP = 2
STAGE_SKEW = 2
ROW_TILE = 688
MIX_TILE = 344
LAST_TILE = 256
MERGE_SPLIT = len(POOL_WINDOWS)
HALO = 16
IN_TILE = 1024
IN_CHUNK = 512
VMEM_LIMIT = 56 * 1024 * 1024

BF16 = jnp.bfloat16
F32 = jnp.float32


def _params(n_axes):
    return pltpu.CompilerParams(
        dimension_semantics=("arbitrary",) * n_axes,
        vmem_limit_bytes=VMEM_LIMIT)


def _sigmoid(x):
    return 1.0 / (1.0 + jnp.exp(-x))


def _rms_scale(x):
    return lax.rsqrt(jnp.mean(x * x, axis=-1, keepdims=True) + RMS_EPS)


def _norm_kernel(x_ref, g_ref, o_ref):
    x = x_ref[0]
    o_ref[0] = (x * _rms_scale(x) * g_ref[0]).astype(o_ref.dtype)


def _prenorm(hs, gains, layer):
    b, l, d = hs.shape
    return pl.pallas_call(
        _norm_kernel,
        grid=(b, l // ROW_TILE),
        in_specs=[pl.BlockSpec((1, ROW_TILE, d), lambda i, r: (i, r, 0)),
                  pl.BlockSpec((1, 1, d), lambda i, r: (layer, 0, 0))],
        out_specs=pl.BlockSpec((1, ROW_TILE, d), lambda i, r: (i, r, 0)),
        out_shape=jax.ShapeDtypeStruct((b, l, d), BF16),
        compiler_params=_params(2),
        name="prenorm",
    )(hs, gains)


def _inproj_kernel(h_ref, w_ref, o_ref):
    for c in range(o_ref.shape[2] // IN_CHUNK):
        cols = slice(c * IN_CHUNK, (c + 1) * IN_CHUNK)
        o_ref[0, :, cols] = jnp.dot(h_ref[0], w_ref[0, :, cols].astype(BF16), preferred_element_type=F32)


def _inproj(h, w_in, layer):
    b, l, d = h.shape
    n = w_in.shape[2]
    return pl.pallas_call(
        _inproj_kernel,
        grid=(b, n // IN_TILE),
        in_specs=[pl.BlockSpec((1, l, d), lambda i, j: (i, 0, 0)),
                  pl.BlockSpec((1, d, IN_TILE), lambda i, j: (layer, 0, j))],
        out_specs=pl.BlockSpec((1, l, IN_TILE), lambda i, j: (i, 0, j)),
        out_shape=jax.ShapeDtypeStruct((b, l, n), F32),
        compiler_params=_params(2),
        name="inproj",
    )(h, w_in)


def _softplus2(y):
    return jnp.maximum(y, jnp.log2(1.0 + jnp.exp2(jnp.minimum(y, SOFTPLUS2_CLAMP))))


def _split_bf16(x):
    hi = x.astype(BF16)
    lo = (x - hi.astype(F32)).astype(BF16)
    return jnp.concatenate([hi, lo], axis=1)


def _attn_kernel(q_ref, k_ref, v_ref, g_ref, o_ref, qb_ref, kb_ref, vb_ref, c_ref, acc_ref, *, seq):
    kblk = KEY_BLOCK
    padded, lanes = qb_ref.shape
    n_tiles = padded // kblk
    heads = [slice(h * HEAD_DIM, (h + 1) * HEAD_DIM) for h in range(lanes // HEAD_DIM)]

    qb_ref[:seq, :] = (q_ref[0] * (HEAD_DIM ** -0.5 * LOG2_E)).astype(BF16)
    kb_ref[:seq, :] = k_ref[0].astype(BF16)
    vb_ref[:seq, :] = v_ref[0].astype(BF16)
    if padded > seq:
        for dst in (qb_ref, kb_ref, vb_ref):
            dst[seq:, :] = jnp.zeros((padded - seq, lanes), BF16)

    wr = lax.broadcasted_iota(jnp.int32, (2 * kblk, 2 * kblk), 0) & (kblk - 1)
    wc = lax.broadcasted_iota(jnp.int32, (2 * kblk, 2 * kblk), 1)
    w = jnp.where((wc >= kblk) | (wr >= wc), 1.0, 0.0).astype(BF16)

    row = lax.broadcasted_iota(jnp.int32, (kblk, kblk), 0)
    col = lax.broadcasted_iota(jnp.int32, (kblk, kblk), 1)
    causal = col < row

    def scores(q, kk):
        return lax.dot_general(q, kk, (((1,), (1,)), ((), ())), preferred_element_type=F32)

    def window(i):
        q0 = i * kblk
        k_lo = max(0, q0 - (WINDOW_BLOCKS - 1) * kblk)
        return q0, k_lo, (q0 + kblk - k_lo) // kblk

    def stage_scores(i, hd):
        q0, k_lo, _ = window(i)
        return scores(qb_ref[q0:q0 + kblk, hd], kb_ref[k_lo:q0 + kblk, hd])

    def stage_sums(i, hd, y):
        nb = window(i)[2]
        sp = _softplus2(y)
        parts = [_split_bf16(sp[:, b * kblk:(b + 1) * kblk]) for b in range(nb - 1)]
        parts.append(_split_bf16(jnp.where(causal, sp[:, (nb - 1) * kblk:], 0.0)))
        return y, jnp.dot(jnp.concatenate(parts, axis=0), w, preferred_element_type=F32)

    def stage_values(i, hd, y, r):
        q0, k_lo, nb = window(i)
        later = jnp.zeros((kblk, kblk), F32)
        probs = [None] * nb
        for b in reversed(range(nb)):
            rb = r[b * kblk:(b + 1) * kblk]
            a = jnp.exp2(y[:, b * kblk:(b + 1) * kblk] - rb[:, :kblk] - later)
            probs[b] = (jnp.where(causal, a, 0.0) if b == nb - 1 else a).astype(BF16)
            later = later + rb[:, kblk:]
        acc_ref[q0:q0 + kblk, hd] = jnp.dot(jnp.concatenate(probs, axis=1),
                                            vb_ref[k_lo:q0 + kblk, hd], preferred_element_type=F32)
        c_ref[q0:q0 + kblk, hd] = later

    units = [(i, hd) for i in range(n_tiles) for hd in heads]
    ys, sums = {}, {}
    for s in range(len(units) + 2 * STAGE_SKEW):
        if s < len(units):
            ys[s] = stage_scores(*units[s])
        if 0 <= s - STAGE_SKEW < len(units):
            sums[s - STAGE_SKEW] = stage_sums(*units[s - STAGE_SKEW], ys.pop(s - STAGE_SKEW))
        if 0 <= s - 2 * STAGE_SKEW < len(units):
            stage_values(*units[s - 2 * STAGE_SKEW], *sums.pop(s - 2 * STAGE_SKEW))

    if n_tiles > WINDOW_BLOCKS:
        first = WINDOW_BLOCKS * kblk

        def block(hd, q, k0, c, acc):
            y = scores(q, kb_ref[pl.ds(k0, kblk), hd])
            sp = _softplus2(y)
            r = jnp.dot(_split_bf16(sp), w, preferred_element_type=F32)
            a = jnp.exp2(y - r[:, :kblk] - c)
            acc = acc + jnp.dot(a.astype(BF16), vb_ref[pl.ds(k0, kblk), hd], preferred_element_type=F32)
            return c + r[:, kblk:], acc

        @pl.when(jnp.min(c_ref[first:, :]) < EXIT_SUM)
        def _():
            def tile(i, _):
                q0 = pl.multiple_of(i * kblk, kblk)
                for hd in heads:
                    q = qb_ref[pl.ds(q0, kblk), hd]
                    c0 = c_ref[pl.ds(q0, kblk), hd]

                    def more(state):
                        return jnp.logical_and(state[0] >= 0, state[1] < EXIT_SUM)

                    def step(state, hd=hd, q=q):
                        j, _, c, acc = state
                        c, acc = block(hd, q, pl.multiple_of(j * kblk, kblk), c, acc)
                        return j - 1, jnp.min(c), c, acc

                    state = (i - WINDOW_BLOCKS, jnp.min(c0), c0, acc_ref[pl.ds(q0, kblk), hd])
                    acc_ref[pl.ds(q0, kblk), hd] = lax.while_loop(more, step, state)[3]
                return 0

            lax.fori_loop(WINDOW_BLOCKS, n_tiles, tile, 0)

    g = g_ref[0]
    o_ref[0] = (acc_ref[:seq, :] * (g * _sigmoid(g))).astype(o_ref.dtype)


def _attention(zin):
    b, l, _ = zin.shape
    width = HEADS * HEAD_DIM
    lanes = HEAD_GROUP * HEAD_DIM
    groups = HEADS // HEAD_GROUP
    padded = pl.cdiv(l, KEY_BLOCK) * KEY_BLOCK
    assert l % 16 == 0

    def spec(part):
        return pl.BlockSpec((1, l, lanes), lambda i, h: (i, 0, part * groups + h))

    return pl.pallas_call(
        functools.partial(_attn_kernel, seq=l),
        grid=(b, groups),
        in_specs=[spec(0), spec(1), spec(2), spec(3)],
        out_specs=pl.BlockSpec((1, l, lanes), lambda i, h: (i, 0, h)),
        out_shape=jax.ShapeDtypeStruct((b, l, width), BF16),
        scratch_shapes=[pltpu.VMEM((padded, lanes), BF16)] * 3 + [pltpu.VMEM((padded, lanes), F32)] * 2,
        compiler_params=_params(2),
        name="stickbreak_attn",
    )(zin, zin, zin, zin)


def _mix_kernel(u_ref, uh_ref, gp_ref, og_ref, ma_ref, mp_ref, x_ref, pw_ref, ps_ref,
                wa_ref, wp_ref, wo_ref, g_ref, *rest, row0):
    out_refs, (op_ref, mg_ref) = rest[:-2], rest[-2:]
    r = pl.program_id(1)
    rows = u_ref.shape[0]
    gd = pw_ref.shape[2]
    d = mg_ref.shape[1]
    chunks = [slice(n * d // MERGE_SPLIT, (n + 1) * d // MERGE_SPLIT) for n in range(MERGE_SPLIT)]

    halo = uh_ref[...]
    if row0 < HALO:
        halo = jnp.where(r == 0, 0.0, halo)
    ext = jnp.concatenate([halo, u_ref[...]], axis=0)
    pos = row0 + r * rows + lax.broadcasted_iota(jnp.int32, (rows, gd), 0)
    y_attn = []
    for g, win in enumerate(POOL_WINDOWS):
        cols = chunks[g]
        y_attn.append(_sigmoid(ma_ref[:, cols])
                      * jnp.dot(og_ref[...], wa_ref[0, :, cols], preferred_element_type=F32))
        sl = slice(g * gd, (g + 1) * gd)
        s = ext[:, sl]
        step = 1
        while step < win:
            s = s + pltpu.roll(s, step, axis=0)
            step *= 2
        cnt = jnp.minimum(pos + 1, win).astype(F32)
        pooled = s[HALO:] / cnt - ext[HALO:, sl]
        mixed = jnp.dot(pooled.astype(BF16), pw_ref[0, g], preferred_element_type=F32)
        gate = gp_ref[:, sl]
        op_ref[:, sl] = (mixed * ps_ref[0, :, sl] * (gate * _sigmoid(gate))).astype(BF16)

    for cols, ya in zip(chunks, y_attn):
        y_pool = jnp.dot(op_ref[...], wp_ref[0, :, cols], preferred_element_type=F32)
        mg_ref[:, cols] = (ya + _sigmoid(mp_ref[:, cols]) * y_pool).astype(BF16)

    x = x_ref[...] + jnp.dot(mg_ref[...], wo_ref[0], preferred_element_type=F32)
    normed = x * _rms_scale(x) * g_ref[0]
    if len(out_refs) == 2:
        out_refs[0][...] = x
    out_refs[-1][...] = normed.astype(out_refs[-1].dtype)


def _mix(zin, og, hs, pool_w, pool_scale, w_attn_up, w_pool_up, w_out, gains, layer, last):
    b, l, d = hs.shape
    pw = og.shape[2]
    u_col, gp_col, ma_col, mp_col = 4 * pw, 5 * pw, 6 * pw, 6 * pw + d
    tile, row0 = (LAST_TILE, N_META) if last else (MIX_TILE, 0)
    out_rows = l - row0
    assert out_rows % tile == 0

    def rows(width, col):
        return pl.BlockSpec((None, pl.Element(tile), pl.Element(width)),
                            lambda i, r: (i, pl.multiple_of(row0 + r * tile, 8), col))

    halo = pl.BlockSpec((None, pl.Element(HALO), pl.Element(pw)),
                        lambda i, r: (i, pl.multiple_of(jnp.maximum(row0 + r * tile - HALO, 0), 8), u_col))

    def resident(arr):
        return pl.BlockSpec((1,) + arr.shape[1:], lambda i, r: (layer,) + (0,) * (arr.ndim - 1))

    out_spec = pl.BlockSpec((None, tile, d), lambda i, r: (i, r, 0))
    out_shape = [jax.ShapeDtypeStruct((b, out_rows, d), F32)]
    if not last:
        out_shape.append(jax.ShapeDtypeStruct((b, out_rows, d), BF16))
    return pl.pallas_call(
        functools.partial(_mix_kernel, row0=row0),
        grid=(b, out_rows // tile),
        in_specs=[rows(pw, u_col), halo, rows(pw, gp_col), rows(pw, 0),
                  rows(d, ma_col), rows(d, mp_col), rows(d, 0),
                  resident(pool_w), resident(pool_scale), resident(w_attn_up), resident(w_pool_up),
                  resident(w_out),
                  pl.BlockSpec((1, 1, d), lambda i, r: (layer + 1, 0, 0))],
        out_specs=[out_spec] * len(out_shape),
        out_shape=out_shape,
        scratch_shapes=[pltpu.VMEM((tile, pw), BF16), pltpu.VMEM((tile, d), BF16)],
        compiler_params=_params(2),
        name="mix_out",
    )(zin, zin, zin, og, zin, zin, hs, pool_w, pool_scale, w_attn_up, w_pool_up, w_out, gains)


def kernel(x, meta_tokens, norm_gain, w_in, pool_w, pool_scale, w_attn_up, w_pool_up, w_out, final_gain):
    b, seq, d = x.shape
    depth = norm_gain.shape[0]
    assert meta_tokens.shape[0] == N_META and (seq + N_META) % ROW_TILE == 0
    meta = jnp.broadcast_to(meta_tokens.astype(x.dtype)[None], (b, N_META, d))
    hs = jnp.concatenate([meta, x], axis=1)

    gains = jnp.concatenate([norm_gain, final_gain[None]], axis=0)[:, None, :]
    pool_scale3 = pool_scale[:, None, :]
    pool_wb = pool_w.astype(BF16)
    wab = w_attn_up.astype(BF16)
    wpb = w_pool_up.astype(BF16)
    wob = w_out.astype(BF16)

    h = _prenorm(hs, gains, 0)
    for layer in range(depth):
        last = layer == depth - 1
        zin = _inproj(h, w_in, layer)
        og = _attention(zin)
        outs = _mix(zin, og, hs, pool_wb, pool_scale3, wab, wpb, wob, gains, layer, last)
        if last:
            return outs[0]
        hs, h = outs
```

```python
import functools

import jax
import jax.numpy as jnp
from jax import lax
from jax.experimental import pallas as pl
from jax.experimental.pallas import tpu as pltpu

N_META = 16
HEADS = 8
HEAD_DIM = 128
POOL_WINDOWS = (2, 4, 8, 16)
RMS_EPS = 1e-6
LOG2_E = 1.4426950408889634
SOFTPLUS2_CLAMP = 126.0
KEY_BLOCK = 128
WINDOW_BLOCKS = 3
EXIT_SUM = 127.0
HEAD_GROUP = 2
STAGE_SKEW = 2
MIX_TILE = 344
LAST_TILE = 256
MERGE_SPLIT = len(POOL_WINDOWS)
HALO = 16
IN_TILE = 1024
IN_CHUNK = 512
CAST_STEPS = 32
VMEM_LIMIT = 56 * 1024 * 1024

BF16 = jnp.bfloat16
F32 = jnp.float32


def _params(n_axes):
    return pltpu.CompilerParams(
        dimension_semantics=("arbitrary",) * n_axes,
        vmem_limit_bytes=VMEM_LIMIT)


def _sigmoid(x):
    return 1.0 / (1.0 + jnp.exp(-x))


def _rms_scale(x):
    return lax.rsqrt(jnp.mean(x * x, axis=-1, keepdims=True) + RMS_EPS)


def _embed_kernel(x_ref, meta_ref, g_ref, hs_ref, h_ref):
    xw = x_ref[...]
    first = jnp.concatenate([meta_ref[...], xw[:-N_META]], axis=0)
    t = jnp.where(pl.program_id(1) == 0, first, xw)
    hs_ref[...] = t
    h_ref[...] = (t * _rms_scale(t) * g_ref[0]).astype(h_ref.dtype)


def _embed(x, meta_tokens, gains):
    b, seq, d = x.shape
    l = seq + N_META
    tile = MIX_TILE
    assert l % tile == 0
    x_spec = pl.BlockSpec(
        (None, pl.Element(tile), pl.Element(d)),
        lambda i, r: (i, pl.multiple_of(jnp.maximum(r * tile - N_META, 0), 8), 0))
    out_spec = pl.BlockSpec((None, tile, d), lambda i, r: (i, r, 0))
    return pl.pallas_call(
        _embed_kernel,
        grid=(b, l // tile),
        in_specs=[x_spec,
                  pl.BlockSpec((N_META, d), lambda i, r: (0, 0)),
                  pl.BlockSpec((1, 1, d), lambda i, r: (0, 0, 0))],
        out_specs=[out_spec, out_spec],
        out_shape=[jax.ShapeDtypeStruct((b, l, d), F32), jax.ShapeDtypeStruct((b, l, d), BF16)],
        compiler_params=_params(2),
        name="embed_prenorm",
    )(x, meta_tokens.astype(x.dtype), gains)


def _inproj_kernel(h_ref, w_ref, *cast_refs):
    n_cast = len(cast_refs) // 2
    o_ref = cast_refs[n_cast]
    for c in range(o_ref.shape[2] // IN_CHUNK):
        cols = slice(c * IN_CHUNK, (c + 1) * IN_CHUNK)
        o_ref[0, :, cols] = jnp.dot(h_ref[0], w_ref[0, :, cols].astype(BF16), preferred_element_type=F32)
    for src, dst in zip(cast_refs[:n_cast], cast_refs[n_cast + 1:]):
        dst[...] = src[0].astype(BF16)


def _inproj(h, w_in, others, layer):
    b, l, d = h.shape
    n = w_in.shape[2]
    steps_per_batch = n // IN_TILE
    assert b * steps_per_batch >= CAST_STEPS

    def chunk(i, j):
        return jnp.minimum(i * steps_per_batch + j, CAST_STEPS - 1)

    cast_in, cast_out, cast_shape = [], [], []
    for w in others:
        rows, cols = w.shape[1:]
        assert rows % (16 * CAST_STEPS) == 0
        cast_in.append(pl.BlockSpec((1, rows // CAST_STEPS, cols), lambda i, j: (layer, chunk(i, j), 0)))
        cast_out.append(pl.BlockSpec((rows // CAST_STEPS, cols), lambda i, j: (chunk(i, j), 0)))
        cast_shape.append(jax.ShapeDtypeStruct((rows, cols), BF16))
    outs = pl.pallas_call(
        _inproj_kernel,
        grid=(b, steps_per_batch),
        in_specs=[pl.BlockSpec((1, l, d), lambda i, j: (i, 0, 0)),
                  pl.BlockSpec((1, d, IN_TILE), lambda i, j: (layer, 0, j))] + cast_in,
        out_specs=[pl.BlockSpec((1, l, IN_TILE), lambda i, j: (i, 0, j))] + cast_out,
        out_shape=[jax.ShapeDtypeStruct((b, l, n), F32)] + cast_shape,
        compiler_params=_params(2),
        name="inproj",
    )(h, w_in, *others)
    return outs[0], outs[1:]


def _softplus2(y):
    return jnp.maximum(y, jnp.log2(1.0 + jnp.exp2(jnp.minimum(y, SOFTPLUS2_CLAMP))))


def _split_bf16(x):
    hi = x.astype(BF16)
    lo = (x - hi.astype(F32)).astype(BF16)
    return jnp.concatenate([hi, lo], axis=1)


def _attn_kernel(q_ref, k_ref, v_ref, g_ref, o_ref, qb_ref, kb_ref, vb_ref, c_ref, acc_ref, *, seq):
    kblk = KEY_BLOCK
    padded, lanes = qb_ref.shape
    n_tiles = padded // kblk
    heads = [slice(h * HEAD_DIM, (h + 1) * HEAD_DIM) for h in range(lanes // HEAD_DIM)]

    qb_ref[:seq, :] = (q_ref[0] * (HEAD_DIM ** -0.5 * LOG2_E)).astype(BF16)
    kb_ref[:seq, :] = k_ref[0].astype(BF16)
    vb_ref[:seq, :] = v_ref[0].astype(BF16)
    if padded > seq:
        for dst in (qb_ref, kb_ref, vb_ref):
            dst[seq:, :] = jnp.zeros((padded - seq, lanes), BF16)

    wr = lax.broadcasted_iota(jnp.int32, (2 * kblk, 2 * kblk), 0) & (kblk - 1)
    wc = lax.broadcasted_iota(jnp.int32, (2 * kblk, 2 * kblk), 1)
    w = jnp.where((wc >= kblk) | (wr >= wc), 1.0, 0.0).astype(BF16)

    row = lax.broadcasted_iota(jnp.int32, (kblk, kblk), 0)
    col = lax.broadcasted_iota(jnp.int32, (kblk, kblk), 1)
    causal = col < row

    def scores(q, kk):
        return lax.dot_general(q, kk, (((1,), (1,)), ((), ())), preferred_element_type=F32)

    def window(i):
        q0 = i * kblk
        k_lo = max(0, q0 - (WINDOW_BLOCKS - 1) * kblk)
        return q0, k_lo, (q0 + kblk - k_lo) // kblk

    def stage_scores(i, hd):
        q0, k_lo, _ = window(i)
        return scores(qb_ref[q0:q0 + kblk, hd], kb_ref[k_lo:q0 + kblk, hd])

    def stage_sums(i, hd, y):
        nb = window(i)[2]
        sp = _softplus2(y)
        parts = [_split_bf16(sp[:, b * kblk:(b + 1) * kblk]) for b in range(nb - 1)]
        parts.append(_split_bf16(jnp.where(causal, sp[:, (nb - 1) * kblk:], 0.0)))
        return y, jnp.dot(jnp.concatenate(parts, axis=0), w, preferred_element_type=F32)

    def stage_values(i, hd, y, r):
        q0, k_lo, nb = window(i)
        later = jnp.zeros((kblk, kblk), F32)
        probs = [None] * nb
        for b in reversed(range(nb)):
            rb = r[b * kblk:(b + 1) * kblk]
            a = jnp.exp2(y[:, b * kblk:(b + 1) * kblk] - rb[:, :kblk] - later)
            probs[b] = (jnp.where(causal, a, 0.0) if b == nb - 1 else a).astype(BF16)
            later = later + rb[:, kblk:]
        acc_ref[q0:q0 + kblk, hd] = jnp.dot(jnp.concatenate(probs, axis=1),
                                            vb_ref[k_lo:q0 + kblk, hd], preferred_element_type=F32)
        c_ref[q0:q0 + kblk, hd] = later

    units = [(i, hd) for i in range(n_tiles) for hd in heads]
    ys, sums = {}, {}
    for s in range(len(units) + 2 * STAGE_SKEW):
        if s < len(units):
            ys[s] = stage_scores(*units[s])
        if 0 <= s - STAGE_SKEW < len(units):
            sums[s - STAGE_SKEW] = stage_sums(*units[s - STAGE_SKEW], ys.pop(s - STAGE_SKEW))
        if 0 <= s - 2 * STAGE_SKEW < len(units):
            stage_values(*units[s - 2 * STAGE_SKEW], *sums.pop(s - 2 * STAGE_SKEW))

    if n_tiles > WINDOW_BLOCKS:
        first = WINDOW_BLOCKS * kblk

        def block(hd, q, k0, c, acc):
            y = scores(q, kb_ref[pl.ds(k0, kblk), hd])
            sp = _softplus2(y)
            r = jnp.dot(_split_bf16(sp), w, preferred_element_type=F32)
            a = jnp.exp2(y - r[:, :kblk] - c)
            acc = acc + jnp.dot(a.astype(BF16), vb_ref[pl.ds(k0, kblk), hd], preferred_element_type=F32)
            return c + r[:, kblk:], acc

        @pl.when(jnp.min(c_ref[first:, :]) < EXIT_SUM)
        def _():
            def tile(i, _):
                q0 = pl.multiple_of(i * kblk, kblk)
                for hd in heads:
                    q = qb_ref[pl.ds(q0, kblk), hd]
                    c0 = c_ref[pl.ds(q0, kblk), hd]

                    def more(state):
                        return jnp.logical_and(state[0] >= 0, state[1] < EXIT_SUM)

                    def step(state, hd=hd, q=q):
                        j, _, c, acc = state
                        c, acc = block(hd, q, pl.multiple_of(j * kblk, kblk), c, acc)
                        return j - 1, jnp.min(c), c, acc

                    state = (i - WINDOW_BLOCKS, jnp.min(c0), c0, acc_ref[pl.ds(q0, kblk), hd])
                    acc_ref[pl.ds(q0, kblk), hd] = lax.while_loop(more, step, state)[3]
                return 0

            lax.fori_loop(WINDOW_BLOCKS, n_tiles, tile, 0)

    g = g_ref[0]
    o_ref[0] = (acc_ref[:seq, :] * (g * _sigmoid(g))).astype(o_ref.dtype)


def _attention(zin):
    b, l, _ = zin.shape
    width = HEADS * HEAD_DIM
    lanes = HEAD_GROUP * HEAD_DIM
    groups = HEADS // HEAD_GROUP
    padded = pl.cdiv(l, KEY_BLOCK) * KEY_BLOCK
    assert l % 16 == 0

    def spec(part):
        return pl.BlockSpec((1, l, lanes), lambda i, h: (i, 0, part * groups + h))

    return pl.pallas_call(
        functools.partial(_attn_kernel, seq=l),
        grid=(b, groups),
        in_specs=[spec(0), spec(1), spec(2), spec(3)],
        out_specs=pl.BlockSpec((1, l, lanes), lambda i, h: (i, 0, h)),
        out_shape=jax.ShapeDtypeStruct((b, l, width), BF16),
        scratch_shapes=[pltpu.VMEM((padded, lanes), BF16)] * 3 + [pltpu.VMEM((padded, lanes), F32)] * 2,
        compiler_params=_params(2),
        name="stickbreak_attn",
    )(zin, zin, zin, zin)


def _mix_kernel(ug_ref, uh_ref, og_ref, m_ref, x_ref, pw_ref, ps_ref,
                wa_ref, wp_ref, wo_ref, g_ref, *rest, row0):
    out_refs, (op_ref, mg_ref) = rest[:-2], rest[-2:]
    r = pl.program_id(1)
    rows, d = mg_ref.shape
    pw = op_ref.shape[1]
    gd = pw_ref.shape[2]
    u_ref, gp_ref = ug_ref.at[:, :pw], ug_ref.at[:, pw:]
    ma_ref, mp_ref = m_ref.at[:, :d], m_ref.at[:, d:]
    chunks = [slice(n * d // MERGE_SPLIT, (n + 1) * d // MERGE_SPLIT) for n in range(MERGE_SPLIT)]

    halo = uh_ref[...]
    if row0 < HALO:
        halo = jnp.where(r == 0, 0.0, halo)
    ext = jnp.concatenate([halo, u_ref[...]], axis=0)
    pos = row0 + r * rows + lax.broadcasted_iota(jnp.int32, (rows, gd), 0)
    y_attn = []
    for g, win in enumerate(POOL_WINDOWS):
        cols = chunks[g]
        y_attn.append(_sigmoid(ma_ref[:, cols])
                      * jnp.dot(og_ref[...], wa_ref[:, cols], preferred_element_type=F32))
        sl = slice(g * gd, (g + 1) * gd)
        s = ext[:, sl]
        step = 1
        while step < win:
            s = s + pltpu.roll(s, step, axis=0)
            step *= 2
        cnt = jnp.minimum(pos + 1, win).astype(F32)
        pooled = s[HALO:] / cnt - ext[HALO:, sl]
        mixed = jnp.dot(pooled.astype(BF16), pw_ref[0, g], preferred_element_type=F32)
        gate = gp_ref[:, sl]
        op_ref[:, sl] = (mixed * ps_ref[0, :, sl] * (gate * _sigmoid(gate))).astype(BF16)

    for cols, ya in zip(chunks, y_attn):
        y_pool = jnp.dot(op_ref[...], wp_ref[:, cols], preferred_element_type=F32)
        mg_ref[:, cols] = (ya + _sigmoid(mp_ref[:, cols]) * y_pool).astype(BF16)

    x = x_ref[...] + jnp.dot(mg_ref[...], wo_ref[...], preferred_element_type=F32)
    normed = x * _rms_scale(x) * g_ref[0]
    if len(out_refs) == 2:
        out_refs[0][...] = x
    out_refs[-1][...] = normed.astype(out_refs[-1].dtype)


def _mix(zin, og, hs, pool_w, pool_scale, w_attn_up, w_pool_up, w_out, gains, layer, last):
    b, l, d = hs.shape
    pw = og.shape[2]
    u_col, m_col = 4 * pw, 6 * pw
    tile, row0 = (LAST_TILE, N_META) if last else (MIX_TILE, 0)
    out_rows = l - row0
    assert out_rows % tile == 0

    def rows(width, col):
        return pl.BlockSpec((None, pl.Element(tile), pl.Element(width)),
                            lambda i, r: (i, pl.multiple_of(row0 + r * tile, 8), col))

    halo = pl.BlockSpec((None, pl.Element(HALO), pl.Element(pw)),
                        lambda i, r: (i, pl.multiple_of(jnp.maximum(row0 + r * tile - HALO, 0), 8), u_col))

    def layer_of(arr):
        return pl.BlockSpec((1,) + arr.shape[1:], lambda i, r: (layer,) + (0,) * (arr.ndim - 1))

    def whole(arr):
        return pl.BlockSpec(arr.shape, lambda i, r: (0,) * arr.ndim)

    out_spec = pl.BlockSpec((None, tile, d), lambda i, r: (i, r, 0))
    out_shape = [jax.ShapeDtypeStruct((b, out_rows, d), F32)]
    if not last:
        out_shape.append(jax.ShapeDtypeStruct((b, out_rows, d), BF16))
    return pl.pallas_call(
        functools.partial(_mix_kernel, row0=row0),
        grid=(b, out_rows // tile),
        in_specs=[rows(2 * pw, u_col), halo, rows(pw, 0), rows(2 * d, m_col), rows(d, 0),
                  layer_of(pool_w), layer_of(pool_scale), whole(w_attn_up), whole(w_pool_up),
                  whole(w_out),
                  pl.BlockSpec((1, 1, d), lambda i, r: (layer + 1, 0, 0))],
        out_specs=[out_spec] * len(out_shape),
        out_shape=out_shape,
        scratch_shapes=[pltpu.VMEM((tile, pw), BF16), pltpu.VMEM((tile, d), BF16)],
        compiler_params=_params(2),
        name="mix_out",
    )(zin, zin, og, zin, hs, pool_w, pool_scale, w_attn_up, w_pool_up, w_out, gains)


def kernel(x, meta_tokens, norm_gain, w_in, pool_w, pool_scale, w_attn_up, w_pool_up, w_out, final_gain):
    depth = norm_gain.shape[0]
    assert meta_tokens.shape[0] == N_META
    gains = jnp.concatenate([norm_gain, final_gain[None]], axis=0)[:, None, :]
    pool_scale3 = pool_scale[:, None, :]
    pool_wb = pool_w.astype(BF16)

    hs, h = _embed(x, meta_tokens, gains)
    for layer in range(depth):
        last = layer == depth - 1
        zin, (wab, wpb, wob) = _inproj(h, w_in, (w_attn_up, w_pool_up, w_out), layer)
        og = _attention(zin)
        outs = _mix(zin, og, hs, pool_wb, pool_scale3, wab, wpb, wob, gains, layer, last)
        if last:
            return outs[0]
        hs, h = outs
```

```python
import functools

import jax
import jax.numpy as jnp
from jax import lax
from jax.experimental import pallas as pl
from jax.experimental.pallas import tpu as pltpu

N_META = 16
HEADS = 8
HEAD_DIM = 128
POOL_WINDOWS = (2, 4, 8, 16)
RMS_EPS = 1e-6
LOG2_E = 1.4426950408889634
SOFTPLUS2_CLAMP = 126.0
KEY_BLOCK = 128
Q_TILE = 64
WINDOW_KEYS = 256
EXIT_SUM = 127.0
HEAD_GROUP = 2
STAGE_SKEW = 4
MIX_TILE = 344
LAST_TILE = 256
MERGE_SPLIT = len(POOL_WINDOWS)
HALO = 16
IN_TILE = 1024
IN_CHUNK = 512
CAST_STEPS = 32
VMEM_LIMIT = 56 * 1024 * 1024

BF16 = jnp.bfloat16
F32 = jnp.float32


def _params(n_axes):
    return pltpu.CompilerParams(
        dimension_semantics=("arbitrary",) * n_axes,
        vmem_limit_bytes=VMEM_LIMIT)


def _sigmoid(x):
    return 1.0 / (1.0 + jnp.exp(-x))


def _rms_scale(x):
    return lax.rsqrt(jnp.mean(x * x, axis=-1, keepdims=True) + RMS_EPS)


def _embed_kernel(x_ref, meta_ref, g_ref, hs_ref, h_ref):
    xw = x_ref[...]
    first = jnp.concatenate([meta_ref[...], xw[:-N_META]], axis=0)
    t = jnp.where(pl.program_id(1) == 0, first, xw)
    hs_ref[...] = t
    h_ref[...] = (t * _rms_scale(t) * g_ref[0]).astype(h_ref.dtype)


def _embed(x, meta_tokens, gains):
    b, seq, d = x.shape
    l = seq + N_META
    tile = MIX_TILE
    assert l % tile == 0
    x_spec = pl.BlockSpec(
        (None, pl.Element(tile), pl.Element(d)),
        lambda i, r: (i, pl.multiple_of(jnp.maximum(r * tile - N_META, 0), 8), 0))
    out_spec = pl.BlockSpec((None, tile, d), lambda i, r: (i, r, 0))
    return pl.pallas_call(
        _embed_kernel,
        grid=(b, l // tile),
        in_specs=[x_spec,
                  pl.BlockSpec((N_META, d), lambda i, r: (0, 0)),
                  pl.BlockSpec((1, 1, d), lambda i, r: (0, 0, 0))],
        out_specs=[out_spec, out_spec],
        out_shape=[jax.ShapeDtypeStruct((b, l, d), F32), jax.ShapeDtypeStruct((b, l, d), BF16)],
        compiler_params=_params(2),
        name="embed_prenorm",
    )(x, meta_tokens.astype(x.dtype), gains)


def _inproj_kernel(h_ref, w_ref, *cast_refs):
    n_cast = len(cast_refs) // 2
    o_ref = cast_refs[n_cast]
    for c in range(o_ref.shape[2] // IN_CHUNK):
        cols = slice(c * IN_CHUNK, (c + 1) * IN_CHUNK)
        o_ref[0, :, cols] = jnp.dot(h_ref[0], w_ref[0, :, cols].astype(BF16), preferred_element_type=F32)
    for src, dst in zip(cast_refs[:n_cast], cast_refs[n_cast + 1:]):
        dst[...] = src[0].astype(BF16)


def _inproj(h, w_in, others, layer):
    b, l, d = h.shape
    n = w_in.shape[2]
    steps_per_batch = n // IN_TILE
    assert b * steps_per_batch >= CAST_STEPS

    def chunk(i, j):
        return jnp.minimum(i * steps_per_batch + j, CAST_STEPS - 1)

    cast_in, cast_out, cast_shape = [], [], []
    for w in others:
        rows, cols = w.shape[1:]
        assert rows % (16 * CAST_STEPS) == 0
        cast_in.append(pl.BlockSpec((1, rows // CAST_STEPS, cols), lambda i, j: (layer, chunk(i, j), 0)))
        cast_out.append(pl.BlockSpec((rows // CAST_STEPS, cols), lambda i, j: (chunk(i, j), 0)))
        cast_shape.append(jax.ShapeDtypeStruct((rows, cols), BF16))
    outs = pl.pallas_call(
        _inproj_kernel,
        grid=(b, steps_per_batch),
        in_specs=[pl.BlockSpec((1, l, d), lambda i, j: (i, 0, 0)),
                  pl.BlockSpec((1, d, IN_TILE), lambda i, j: (layer, 0, j))] + cast_in,
        out_specs=[pl.BlockSpec((1, l, IN_TILE), lambda i, j: (i, 0, j))] + cast_out,
        out_shape=[jax.ShapeDtypeStruct((b, l, n), F32)] + cast_shape,
        compiler_params=_params(2),
        name="inproj",
    )(h, w_in, *others)
    return outs[0], outs[1:]


def _softplus2(y):
    return jnp.maximum(y, jnp.log2(1.0 + jnp.exp2(jnp.minimum(y, SOFTPLUS2_CLAMP))))


def _split_bf16(x):
    hi = x.astype(BF16)
    lo = (x - hi.astype(F32)).astype(BF16)
    return jnp.concatenate([hi, lo], axis=1)


def _attn_kernel(q_ref, k_ref, v_ref, g_ref, o_ref, qb_ref, kb_ref, vb_ref, c_ref, acc_ref, *, seq):
    kblk, tq = KEY_BLOCK, Q_TILE
    padded, lanes = qb_ref.shape
    n_tiles = padded // tq
    heads = [slice(h * HEAD_DIM, (h + 1) * HEAD_DIM) for h in range(lanes // HEAD_DIM)]

    qb_ref[:seq, :] = (q_ref[0] * (HEAD_DIM ** -0.5 * LOG2_E)).astype(BF16)
    kb_ref[:seq, :] = k_ref[0].astype(BF16)
    vb_ref[:seq, :] = v_ref[0].astype(BF16)
    if padded > seq:
        for dst in (qb_ref, kb_ref, vb_ref):
            dst[seq:, :] = jnp.zeros((padded - seq, lanes), BF16)

    wr = lax.broadcasted_iota(jnp.int32, (2 * kblk, 2 * kblk), 0) & (kblk - 1)
    wc = lax.broadcasted_iota(jnp.int32, (2 * kblk, 2 * kblk), 1)
    w = jnp.where((wc >= kblk) | (wr >= wc), 1.0, 0.0).astype(BF16)

    row = lax.broadcasted_iota(jnp.int32, (tq, kblk), 0)
    col = lax.broadcasted_iota(jnp.int32, (tq, kblk), 1)

    def scores(q, kk):
        return lax.dot_general(q, kk, (((1,), (1,)), ((), ())), preferred_element_type=F32)

    def window(i):
        q0 = i * tq
        k_lo = max(0, q0 + tq - WINDOW_KEYS)
        nb = pl.cdiv(q0 + tq - k_lo, kblk)
        shift = q0 - k_lo - (nb - 1) * kblk
        assert 0 <= shift and k_lo + nb * kblk <= padded
        return q0, k_lo, nb, col < row + shift

    def stage_scores(i, hd):
        q0, k_lo, nb, _ = window(i)
        return scores(qb_ref[q0:q0 + tq, hd], kb_ref[k_lo:k_lo + nb * kblk, hd])

    def stage_sums(i, hd, y):
        _, _, nb, past = window(i)
        sp = _softplus2(y)
        parts = [_split_bf16(sp[:, b * kblk:(b + 1) * kblk]) for b in range(nb - 1)]
        parts.append(_split_bf16(jnp.where(past, sp[:, (nb - 1) * kblk:], 0.0)))
        return y, jnp.dot(jnp.concatenate(parts, axis=0), w, preferred_element_type=F32)

    def stage_values(i, hd, y, r):
        q0, k_lo, nb, past = window(i)
        later = jnp.zeros((tq, kblk), F32)
        probs = [None] * nb
        for b in reversed(range(nb)):
            rb = r[b * tq:(b + 1) * tq]
            a = jnp.exp2(y[:, b * kblk:(b + 1) * kblk] - rb[:, :kblk] - later)
            probs[b] = (jnp.where(past, a, 0.0) if b == nb - 1 else a).astype(BF16)
            later = later + rb[:, kblk:]
        acc_ref[q0:q0 + tq, hd] = jnp.dot(jnp.concatenate(probs, axis=1),
                                          vb_ref[k_lo:k_lo + nb * kblk, hd], preferred_element_type=F32)
        c_ref[q0:q0 + tq, hd] = later

    units = [(i, hd) for i in range(n_tiles) for hd in heads]
    ys, sums = {}, {}
    for s in range(len(units) + 2 * STAGE_SKEW):
        if s < len(units):
            ys[s] = stage_scores(*units[s])
        if 0 <= s - STAGE_SKEW < len(units):
            sums[s - STAGE_SKEW] = stage_sums(*units[s - STAGE_SKEW], ys.pop(s - STAGE_SKEW))
        if 0 <= s - 2 * STAGE_SKEW < len(units):
            stage_values(*units[s - 2 * STAGE_SKEW], *sums.pop(s - 2 * STAGE_SKEW))

    first_tile = WINDOW_KEYS // tq
    if n_tiles > first_tile:

        def block(hd, q, k0, n_new, c, acc):
            new = col < n_new
            y = scores(q, kb_ref[pl.ds(k0, kblk), hd])
            r = jnp.dot(_split_bf16(jnp.where(new, _softplus2(y), 0.0)), w, preferred_element_type=F32)
            a = jnp.where(new, jnp.exp2(y - r[:, :kblk] - c), 0.0)
            acc = acc + jnp.dot(a.astype(BF16), vb_ref[pl.ds(k0, kblk), hd], preferred_element_type=F32)
            return c + r[:, kblk:], acc

        @pl.when(jnp.min(c_ref[first_tile * tq:, :]) < EXIT_SUM)
        def _():
            def tile(i, _):
                q0 = pl.multiple_of(i * tq, tq)
                k_lo = q0 + tq - WINDOW_KEYS
                n_blocks = (k_lo + kblk - 1) // kblk
                for hd in heads:
                    q = qb_ref[pl.ds(q0, tq), hd]
                    c0 = c_ref[pl.ds(q0, tq), hd]

                    def more(state):
                        return jnp.logical_and(state[0] < n_blocks, state[1] < EXIT_SUM)

                    def step(state, hd=hd, q=q):
                        j, _, c, acc = state
                        top = k_lo - j * kblk
                        k0 = jnp.maximum(top - kblk, 0)
                        c, acc = block(hd, q, pl.multiple_of(k0, tq), top - k0, c, acc)
                        return j + 1, jnp.min(c), c, acc

                    state = (0, jnp.min(c0), c0, acc_ref[pl.ds(q0, tq), hd])
                    acc_ref[pl.ds(q0, tq), hd] = lax.while_loop(more, step, state)[3]
                return 0

            lax.fori_loop(first_tile, n_tiles, tile, 0)

    g = g_ref[0]
    o_ref[0] = (acc_ref[:seq, :] * (g * _sigmoid(g))).astype(o_ref.dtype)


def _attention(zin):
    b, l, _ = zin.shape
    width = HEADS * HEAD_DIM
    lanes = HEAD_GROUP * HEAD_DIM
    groups = HEADS // HEAD_GROUP
    padded = pl.cdiv(l, Q_TILE) * Q_TILE
    assert l % 16 == 0 and WINDOW_KEYS % KEY_BLOCK == 0 and KEY_BLOCK % Q_TILE == 0

    def spec(part):
        return pl.BlockSpec((1, l, lanes), lambda i, h: (i, 0, part * groups + h))

    return pl.pallas_call(
        functools.partial(_attn_kernel, seq=l),
        grid=(b, groups),
        in_specs=[spec(0), spec(1), spec(2), spec(3)],
        out_specs=pl.BlockSpec((1, l, lanes), lambda i, h: (i, 0, h)),
        out_shape=jax.ShapeDtypeStruct((b, l, width), BF16),
        scratch_shapes=[pltpu.VMEM((padded, lanes), BF16)] * 3 + [pltpu.VMEM((padded, lanes), F32)] * 2,
        compiler_params=_params(2),
        name="stickbreak_attn",
    )(zin, zin, zin, zin)


def _mix_kernel(ug_ref, uh_ref, og_ref, m_ref, x_ref, pw_ref, ps_ref,
                wa_ref, wp_ref, wo_ref, g_ref, *rest, row0):
    out_refs, (op_ref, mg_ref) = rest[:-2], rest[-2:]
    r = pl.program_id(1)
    rows, d = mg_ref.shape
    pw = op_ref.shape[1]
    gd = pw_ref.shape[2]
    u_ref, gp_ref = ug_ref.at[:, :pw], ug_ref.at[:, pw:]
    ma_ref, mp_ref = m_ref.at[:, :d], m_ref.at[:, d:]
    chunks = [slice(n * d // MERGE_SPLIT, (n + 1) * d // MERGE_SPLIT) for n in range(MERGE_SPLIT)]

    halo = uh_ref[...]
    if row0 < HALO:
        halo = jnp.where(r == 0, 0.0, halo)
    ext = jnp.concatenate([halo, u_ref[...]], axis=0)
    pos = row0 + r * rows + lax.broadcasted_iota(jnp.int32, (rows, gd), 0)
    y_attn = []
    for g, win in enumerate(POOL_WINDOWS):
        cols = chunks[g]
        y_attn.append(_sigmoid(ma_ref[:, cols])
                      * jnp.dot(og_ref[...], wa_ref[:, cols], preferred_element_type=F32))
        sl = slice(g * gd, (g + 1) * gd)
        s = ext[:, sl]
        step = 1
        while step < win:
            s = s + pltpu.roll(s, step, axis=0)
            step *= 2
        cnt = jnp.minimum(pos + 1, win).astype(F32)
        pooled = s[HALO:] / cnt - ext[HALO:, sl]
        mixed = jnp.dot(pooled.astype(BF16), pw_ref[0, g], preferred_element_type=F32)
        gate = gp_ref[:, sl]
        op_ref[:, sl] = (mixed * ps_ref[0, :, sl] * (gate * _sigmoid(gate))).astype(BF16)

    for cols, ya in zip(chunks, y_attn):
        y_pool = jnp.dot(op_ref[...], wp_ref[:, cols], preferred_element_type=F32)
        mg_ref[:, cols] = (ya + _sigmoid(mp_ref[:, cols]) * y_pool).astype(BF16)

    x = x_ref[...] + jnp.dot(mg_ref[...], wo_ref[...], preferred_element_type=F32)
    normed = x * _rms_scale(x) * g_ref[0]
    if len(out_refs) == 2:
        out_refs[0][...] = x
    out_refs[-1][...] = normed.astype(out_refs[-1].dtype)


def _mix(zin, og, hs, pool_w, pool_scale, w_attn_up, w_pool_up, w_out, gains, layer, last):
    b, l, d = hs.shape
    pw = og.shape[2]
    u_col, m_col = 4 * pw, 6 * pw
    tile, row0 = (LAST_TILE, N_META) if last else (MIX_TILE, 0)
    out_rows = l - row0
    assert out_rows % tile == 0

    def rows(width, col):
        return pl.BlockSpec((None, pl.Element(tile), pl.Element(width)),
                            lambda i, r: (i, pl.multiple_of(row0 + r * tile, 8), col))

    halo = pl.BlockSpec((None, pl.Element(HALO), pl.Element(pw)),
                        lambda i, r: (i, pl.multiple_of(jnp.maximum(row0 + r * tile - HALO, 0), 8), u_col))

    def layer_of(arr):
        return pl.BlockSpec((1,) + arr.shape[1:], lambda i, r: (layer,) + (0,) * (arr.ndim - 1))

    def whole(arr):
        return pl.BlockSpec(arr.shape, lambda i, r: (0,) * arr.ndim)

    out_spec = pl.BlockSpec((None, tile, d), lambda i, r: (i, r, 0))
    out_shape = [jax.ShapeDtypeStruct((b, out_rows, d), F32)]
    if not last:
        out_shape.append(jax.ShapeDtypeStruct((b, out_rows, d), BF16))
    return pl.pallas_call(
        functools.partial(_mix_kernel, row0=row0),
        grid=(b, out_rows // tile),
        in_specs=[rows(2 * pw, u_col), halo, rows(pw, 0), rows(2 * d, m_col), rows(d, 0),
                  layer_of(pool_w), layer_of(pool_scale), whole(w_attn_up), whole(w_pool_up),
                  whole(w_out),
                  pl.BlockSpec((1, 1, d), lambda i, r: (layer + 1, 0, 0))],
        out_specs=[out_spec] * len(out_shape),
        out_shape=out_shape,
        scratch_shapes=[pltpu.VMEM((tile, pw), BF16), pltpu.VMEM((tile, d), BF16)],
        compiler_params=_params(2),
        name="mix_out",
    )(zin, zin, og, zin, hs, pool_w, pool_scale, w_attn_up, w_pool_up, w_out, gains)


def kernel(x, meta_tokens, norm_gain, w_in, pool_w, pool_scale, w_attn_up, w_pool_up, w_out, final_gain):
    depth = norm_gain.shape[0]
    assert meta_tokens.shape[0] == N_META
    gains = jnp.concatenate([norm_gain, final_gain[None]], axis=0)[:, None, :]
    pool_scale3 = pool_scale[:, None, :]
    pool_wb = pool_w.astype(BF16)

    hs, h = _embed(x, meta_tokens, gains)
    for layer in range(depth):
        last = layer == depth - 1
        zin, (wab, wpb, wob) = _inproj(h, w_in, (w_attn_up, w_pool_up, w_out), layer)
        og = _attention(zin)
        outs = _mix(zin, og, hs, pool_wb, pool_scale3, wab, wpb, wob, gains, layer, last)
        if last:
            return outs[0]
        hs, h = outs
```

```python
import functools

import jax
import jax.numpy as jnp
from jax import lax
from jax.experimental import pallas as pl
from jax.experimental.pallas import tpu as pltpu

N_META = 16
HEADS = 8
HEAD_DIM = 128
POOL_WINDOWS = (2, 4, 8, 16)
RMS_EPS = 1e-6
LOG2_E = 1.4426950408889634
QUERY_SCALE = HEAD_DIM ** -0.5 * LOG2_E
SOFTPLUS2_CLAMP = 126.0
KEY_BLOCK = 128
Q_TILE = 64
WINDOW_KEYS = 256
EXIT_SUM = 127.0
HEAD_GROUP = 2
STAGE_SKEW = 4
MIX_TILE = 344
LAST_TILE = 256
MERGE_SPLIT = len(POOL_WINDOWS)
HALO = 16
IN_TILE = 1024
IN_CHUNK = 512
CAST_CHUNK_BYTES = 1 << 20
VMEM_LIMIT = 56 * 1024 * 1024

BF16 = jnp.bfloat16
F32 = jnp.float32


def _params(n_axes):
    return pltpu.CompilerParams(
        dimension_semantics=("arbitrary",) * n_axes,
        vmem_limit_bytes=VMEM_LIMIT)


def _sigmoid(x):
    return 1.0 / (1.0 + jnp.exp(-x))


def _rms_scale(x):
    return lax.rsqrt(jnp.mean(x * x, axis=-1, keepdims=True) + RMS_EPS)


def _embed_kernel(x_ref, meta_ref, g_ref, hs_ref, h_ref):
    xw = x_ref[...]
    first = jnp.concatenate([meta_ref[...], xw[:-N_META]], axis=0)
    t = jnp.where(pl.program_id(1) == 0, first, xw)
    hs_ref[...] = t
    h_ref[...] = (t * _rms_scale(t) * g_ref[0]).astype(h_ref.dtype)


def _embed(x, meta_tokens, gains):
    b, seq, d = x.shape
    l = seq + N_META
    tile = MIX_TILE
    assert l % tile == 0
    x_spec = pl.BlockSpec(
        (None, pl.Element(tile), pl.Element(d)),
        lambda i, r: (i, pl.multiple_of(jnp.maximum(r * tile - N_META, 0), 8), 0))
    out_spec = pl.BlockSpec((None, tile, d), lambda i, r: (i, r, 0))
    return pl.pallas_call(
        _embed_kernel,
        grid=(b, l // tile),
        in_specs=[x_spec,
                  pl.BlockSpec((N_META, d), lambda i, r: (0, 0)),
                  pl.BlockSpec((1, 1, d), lambda i, r: (0, 0, 0))],
        out_specs=[out_spec, out_spec],
        out_shape=[jax.ShapeDtypeStruct((b, l, d), F32), jax.ShapeDtypeStruct((b, l, d), BF16)],
        compiler_params=_params(2),
        name="embed_prenorm",
    )(x, meta_tokens.astype(x.dtype), gains)


def _inproj_kernel(h_ref, w_ref, *cast_refs, first_tile_scale):
    n_cast = len(cast_refs) // 2
    o_ref = cast_refs[n_cast]
    scale = jnp.where(pl.program_id(1) == 0, first_tile_scale, 1.0)
    for c in range(o_ref.shape[2] // IN_CHUNK):
        cols = slice(c * IN_CHUNK, (c + 1) * IN_CHUNK)
        res = jnp.dot(h_ref[0], w_ref[0, :, cols].astype(BF16), preferred_element_type=F32)
        if first_tile_scale != 1.0:
            res = res * scale
        o_ref[0, :, cols] = res.astype(o_ref.dtype)
    for src, dst in zip(cast_refs[:n_cast], cast_refs[n_cast + 1:]):
        dst[...] = src[0].astype(BF16)


def _inproj(h, w_in, layer, col0, n, out_dtype, first_tile_scale=1.0, others=()):
    b, l, d = h.shape
    steps_per_batch = n // IN_TILE
    tile0 = col0 // IN_TILE
    assert n % IN_TILE == 0 and col0 % IN_TILE == 0

    cast_in, cast_out, cast_shape = [], [], []
    for w in others:
        rows, cols = w.shape[1:]
        chunk_rows = CAST_CHUNK_BYTES // (cols * 4)
        n_chunks = rows // chunk_rows
        assert rows % chunk_rows == 0 and chunk_rows % 16 == 0 and n_chunks <= b * steps_per_batch

        def chunk(i, j, n_chunks=n_chunks):
            return jnp.minimum(i * steps_per_batch + j, n_chunks - 1)

        cast_in.append(pl.BlockSpec((1, chunk_rows, cols), lambda i, j, c=chunk: (layer, c(i, j), 0)))
        cast_out.append(pl.BlockSpec((chunk_rows, cols), lambda i, j, c=chunk: (c(i, j), 0)))
        cast_shape.append(jax.ShapeDtypeStruct((rows, cols), BF16))
    outs = pl.pallas_call(
        functools.partial(_inproj_kernel, first_tile_scale=first_tile_scale),
        grid=(b, steps_per_batch),
        in_specs=[pl.BlockSpec((1, l, d), lambda i, j: (i, 0, 0)),
                  pl.BlockSpec((1, d, IN_TILE), lambda i, j: (layer, 0, tile0 + j))] + cast_in,
        out_specs=[pl.BlockSpec((1, l, IN_TILE), lambda i, j: (i, 0, j))] + cast_out,
        out_shape=[jax.ShapeDtypeStruct((b, l, n), out_dtype)] + cast_shape,
        compiler_params=_params(2),
        name="inproj",
    )(h, w_in, *others)
    return outs[0], outs[1:]


def _softplus2(y):
    return jnp.maximum(y, jnp.log2(1.0 + jnp.exp2(jnp.minimum(y, SOFTPLUS2_CLAMP))))


def _split_bf16(x):
    hi = x.astype(BF16)
    lo = (x - hi.astype(F32)).astype(BF16)
    return jnp.concatenate([hi, lo], axis=1)


def _attn_kernel(q_ref, k_ref, v_ref, g_ref, o_ref, qb_ref, kb_ref, vb_ref, c_ref, acc_ref, *, seq):
    kblk, tq = KEY_BLOCK, Q_TILE
    padded, lanes = qb_ref.shape
    n_tiles = padded // tq
    heads = [slice(h * HEAD_DIM, (h + 1) * HEAD_DIM) for h in range(lanes // HEAD_DIM)]

    for src, dst in ((q_ref, qb_ref), (k_ref, kb_ref), (v_ref, vb_ref)):
        dst[:seq, :] = src[0]
        if padded > seq:
            dst[seq:, :] = jnp.zeros((padded - seq, lanes), BF16)

    wr = lax.broadcasted_iota(jnp.int32, (2 * kblk, 2 * kblk), 0) & (kblk - 1)
    wc = lax.broadcasted_iota(jnp.int32, (2 * kblk, 2 * kblk), 1)
    w = jnp.where((wc >= kblk) | (wr >= wc), 1.0, 0.0).astype(BF16)

    row = lax.broadcasted_iota(jnp.int32, (tq, kblk), 0)
    col = lax.broadcasted_iota(jnp.int32, (tq, kblk), 1)

    def scores(q, kk):
        return lax.dot_general(q, kk, (((1,), (1,)), ((), ())), preferred_element_type=F32)

    def window(i):
        q0 = i * tq
        k_lo = max(0, q0 + tq - WINDOW_KEYS)
        nb = pl.cdiv(q0 + tq - k_lo, kblk)
        shift = q0 - k_lo - (nb - 1) * kblk
        assert 0 <= shift and k_lo + nb * kblk <= padded
        return q0, k_lo, nb, col < row + shift

    def stage_scores(i, hd):
        q0, k_lo, nb, _ = window(i)
        return scores(qb_ref[q0:q0 + tq, hd], kb_ref[k_lo:k_lo + nb * kblk, hd])

    def stage_sums(i, hd, y):
        _, _, nb, past = window(i)
        sp = _softplus2(y)
        parts = [_split_bf16(sp[:, b * kblk:(b + 1) * kblk]) for b in range(nb - 1)]
        parts.append(_split_bf16(jnp.where(past, sp[:, (nb - 1) * kblk:], 0.0)))
        return y, jnp.dot(jnp.concatenate(parts, axis=0), w, preferred_element_type=F32)

    def stage_values(i, hd, y, r):
        q0, k_lo, nb, past = window(i)
        later = jnp.zeros((tq, kblk), F32)
        probs = [None] * nb
        for b in reversed(range(nb)):
            rb = r[b * tq:(b + 1) * tq]
            a = jnp.exp2(y[:, b * kblk:(b + 1) * kblk] - rb[:, :kblk] - later)
            probs[b] = (jnp.where(past, a, 0.0) if b == nb - 1 else a).astype(BF16)
            later = later + rb[:, kblk:]
        acc_ref[q0:q0 + tq, hd] = jnp.dot(jnp.concatenate(probs, axis=1),
                                          vb_ref[k_lo:k_lo + nb * kblk, hd], preferred_element_type=F32)
        c_ref[q0:q0 + tq, hd] = later

    units = [(i, hd) for i in range(n_tiles) for hd in heads]
    ys, sums = {}, {}
    for s in range(len(units) + 2 * STAGE_SKEW):
        if s < len(units):
            ys[s] = stage_scores(*units[s])
        if 0 <= s - STAGE_SKEW < len(units):
            sums[s - STAGE_SKEW] = stage_sums(*units[s - STAGE_SKEW], ys.pop(s - STAGE_SKEW))
        if 0 <= s - 2 * STAGE_SKEW < len(units):
            stage_values(*units[s - 2 * STAGE_SKEW], *sums.pop(s - 2 * STAGE_SKEW))

    first_tile = WINDOW_KEYS // tq
    if n_tiles > first_tile:

        def block(hd, q, k0, n_new, c, acc):
            new = col < n_new
            y = scores(q, kb_ref[pl.ds(k0, kblk), hd])
            r = jnp.dot(_split_bf16(jnp.where(new, _softplus2(y), 0.0)), w, preferred_element_type=F32)
            a = jnp.where(new, jnp.exp2(y - r[:, :kblk] - c), 0.0)
            acc = acc + jnp.dot(a.astype(BF16), vb_ref[pl.ds(k0, kblk), hd], preferred_element_type=F32)
            return c + r[:, kblk:], acc

        @pl.when(jnp.min(c_ref[first_tile * tq:, :]) < EXIT_SUM)
        def _():
            def tile(i, _):
                q0 = pl.multiple_of(i * tq, tq)
                k_lo = q0 + tq - WINDOW_KEYS
                n_blocks = (k_lo + kblk - 1) // kblk
                for hd in heads:
                    q = qb_ref[pl.ds(q0, tq), hd]
                    c0 = c_ref[pl.ds(q0, tq), hd]

                    def more(state):
                        return jnp.logical_and(state[0] < n_blocks, state[1] < EXIT_SUM)

                    def step(state, hd=hd, q=q):
                        j, _, c, acc = state
                        top = k_lo - j * kblk
                        k0 = jnp.maximum(top - kblk, 0)
                        c, acc = block(hd, q, pl.multiple_of(k0, tq), top - k0, c, acc)
                        return j + 1, jnp.min(c), c, acc

                    state = (0, jnp.min(c0), c0, acc_ref[pl.ds(q0, tq), hd])
                    acc_ref[pl.ds(q0, tq), hd] = lax.while_loop(more, step, state)[3]
                return 0

            lax.fori_loop(first_tile, n_tiles, tile, 0)

    g = g_ref[0]
    o_ref[0] = (acc_ref[:seq, :] * (g * _sigmoid(g))).astype(o_ref.dtype)


def _attention(qkv, rest):
    b, l, _ = qkv.shape
    width = HEADS * HEAD_DIM
    lanes = HEAD_GROUP * HEAD_DIM
    groups = HEADS // HEAD_GROUP
    padded = pl.cdiv(l, Q_TILE) * Q_TILE
    assert l % 16 == 0 and WINDOW_KEYS % KEY_BLOCK == 0 and KEY_BLOCK % Q_TILE == 0

    def spec(part):
        return pl.BlockSpec((1, l, lanes), lambda i, h: (i, 0, part * groups + h))

    return pl.pallas_call(
        functools.partial(_attn_kernel, seq=l),
        grid=(b, groups),
        in_specs=[spec(0), spec(1), spec(2), spec(0)],
        out_specs=pl.BlockSpec((1, l, lanes), lambda i, h: (i, 0, h)),
        out_shape=jax.ShapeDtypeStruct((b, l, width), BF16),
        scratch_shapes=[pltpu.VMEM((padded, lanes), BF16)] * 3 + [pltpu.VMEM((padded, lanes), F32)] * 2,
        compiler_params=_params(2),
        name="stickbreak_attn",
    )(qkv, qkv, qkv, rest)


def _mix_kernel(ug_ref, uh_ref, og_ref, m_ref, x_ref, pw_ref, ps_ref,
                wa_ref, wp_ref, wo_ref, g_ref, *rest, row0):
    out_refs, (op_ref, mg_ref) = rest[:-2], rest[-2:]
    r = pl.program_id(1)
    rows, d = mg_ref.shape
    pw = op_ref.shape[1]
    gd = pw_ref.shape[2]
    u_ref, gp_ref = ug_ref.at[:, :pw], ug_ref.at[:, pw:]
    ma_ref, mp_ref = m_ref.at[:, :d], m_ref.at[:, d:]
    chunks = [slice(n * d // MERGE_SPLIT, (n + 1) * d // MERGE_SPLIT) for n in range(MERGE_SPLIT)]

    halo = uh_ref[...]
    if row0 < HALO:
        halo = jnp.where(r == 0, 0.0, halo)
    ext = jnp.concatenate([halo, u_ref[...]], axis=0)
    pos = row0 + r * rows + lax.broadcasted_iota(jnp.int32, (rows, gd), 0)
    y_attn = []
    for g, win in enumerate(POOL_WINDOWS):
        cols = chunks[g]
        y_attn.append(_sigmoid(ma_ref[:, cols])
                      * jnp.dot(og_ref[...], wa_ref[:, cols], preferred_element_type=F32))
        sl = slice(g * gd, (g + 1) * gd)
        s = ext[:, sl]
        step = 1
        while step < win:
            s = s + pltpu.roll(s, step, axis=0)
            step *= 2
        cnt = jnp.minimum(pos + 1, win).astype(F32)
        pooled = s[HALO:] / cnt - ext[HALO:, sl]
        mixed = jnp.dot(pooled.astype(BF16), pw_ref[0, g], preferred_element_type=F32)
        gate = gp_ref[:, sl]
        op_ref[:, sl] = (mixed * ps_ref[0, :, sl] * (gate * _sigmoid(gate))).astype(BF16)

    for cols, ya in zip(chunks, y_attn):
        y_pool = jnp.dot(op_ref[...], wp_ref[:, cols], preferred_element_type=F32)
        mg_ref[:, cols] = (ya + _sigmoid(mp_ref[:, cols]) * y_pool).astype(BF16)

    x = x_ref[...] + jnp.dot(mg_ref[...], wo_ref[...], preferred_element_type=F32)
    normed = x * _rms_scale(x) * g_ref[0]
    if len(out_refs) == 2:
        out_refs[0][...] = x
    out_refs[-1][...] = normed.astype(out_refs[-1].dtype)


def _mix(zin, og, hs, pool_w, pool_scale, w_attn_up, w_pool_up, w_out, gains, layer, last):
    b, l, d = hs.shape
    pw = og.shape[2]
    u_col, m_col = pw, 3 * pw
    tile, row0 = (LAST_TILE, N_META) if last else (MIX_TILE, 0)
    out_rows = l - row0
    assert out_rows % tile == 0

    def rows(width, col):
        return pl.BlockSpec((None, pl.Element(tile), pl.Element(width)),
                            lambda i, r: (i, pl.multiple_of(row0 + r * tile, 8), col))

    halo = pl.BlockSpec((None, pl.Element(HALO), pl.Element(pw)),
                        lambda i, r: (i, pl.multiple_of(jnp.maximum(row0 + r * tile - HALO, 0), 8), u_col))

    def layer_of(arr):
        return pl.BlockSpec((1,) + arr.shape[1:], lambda i, r: (layer,) + (0,) * (arr.ndim - 1))

    def whole(arr):
        return pl.BlockSpec(arr.shape, lambda i, r: (0,) * arr.ndim)

    out_spec = pl.BlockSpec((None, tile, d), lambda i, r: (i, r, 0))
    out_shape = [jax.ShapeDtypeStruct((b, out_rows, d), F32)]
    if not last:
        out_shape.append(jax.ShapeDtypeStruct((b, out_rows, d), BF16))
    return pl.pallas_call(
        functools.partial(_mix_kernel, row0=row0),
        grid=(b, out_rows // tile),
        in_specs=[rows(2 * pw, u_col), halo, rows(pw, 0), rows(2 * d, m_col), rows(d, 0),
                  layer_of(pool_w), layer_of(pool_scale), whole(w_attn_up), whole(w_pool_up),
                  whole(w_out),
                  pl.BlockSpec((1, 1, d), lambda i, r: (layer + 1, 0, 0))],
        out_specs=[out_spec] * len(out_shape),
        out_shape=out_shape,
        scratch_shapes=[pltpu.VMEM((tile, pw), BF16), pltpu.VMEM((tile, d), BF16)],
        compiler_params=_params(2),
        name="mix_out",
    )(zin, zin, og, zin, hs, pool_w, pool_scale, w_attn_up, w_pool_up, w_out, gains)


def kernel(x, meta_tokens, norm_gain, w_in, pool_w, pool_scale, w_attn_up, w_pool_up, w_out, final_gain):
    depth = norm_gain.shape[0]
    assert meta_tokens.shape[0] == N_META
    gains = jnp.concatenate([norm_gain, final_gain[None]], axis=0)[:, None, :]
    pool_scale3 = pool_scale[:, None, :]
    pool_wb = pool_w.astype(BF16)

    hs, h = _embed(x, meta_tokens, gains)
    for layer in range(depth):
        last = layer == depth - 1
        n_qkv = 3 * HEADS * HEAD_DIM
        qkv, (wab, wpb) = _inproj(h, w_in, layer, 0, n_qkv, BF16, first_tile_scale=QUERY_SCALE,
                                  others=(w_attn_up, w_pool_up))
        zin, (wob,) = _inproj(h, w_in, layer, n_qkv, w_in.shape[2] - n_qkv, F32, others=(w_out,))
        og = _attention(qkv, zin)
        outs = _mix(zin, og, hs, pool_wb, pool_scale3, wab, wpb, wob, gains, layer, last)
        if last:
            return outs[0]
        hs, h = outs
```

```python
import functools

import jax
import jax.numpy as jnp
from jax import lax
from jax.experimental import pallas as pl
from jax.experimental.pallas import tpu as pltpu

N_META = 16
HEADS = 8
HEAD_DIM = 128
POOL_WINDOWS = (2, 4, 8, 16)
RMS_EPS = 1e-6
LOG2_E = 1.4426950408889634
SOFTPLUS2_CLAMP = 126.0
KEY_BLOCK = 128
Q_TILE = 64
WINDOW_KEYS = 256
EXIT_SUM = 127.0
HEAD_GROUP = 2
STAGE_SKEW = 4
MIX_TILE = 344
LAST_TILE = 256
MERGE_SPLIT = len(POOL_WINDOWS)
HALO = 16
IN_TILE = 1024
IN_CHUNK = 512
CAST_STEPS = 32
VMEM_LIMIT = 56 * 1024 * 1024

BF16 = jnp.bfloat16
F32 = jnp.float32


def _params(n_axes):
    return pltpu.CompilerParams(
        dimension_semantics=("arbitrary",) * n_axes,
        vmem_limit_bytes=VMEM_LIMIT)


def _sigmoid(x):
    return 1.0 / (1.0 + jnp.exp(-x))


def _rms_scale(x):
    return lax.rsqrt(jnp.mean(x * x, axis=-1, keepdims=True) + RMS_EPS)


def _stream_tile(x_ref, meta_ref, tile_index):
    xw = x_ref[...]
    first = jnp.concatenate([meta_ref[...], xw[:-N_META]], axis=0)
    return jnp.where(tile_index == 0, first, xw)


def _stream_specs(tile, d):
    x_spec = pl.BlockSpec(
        (None, pl.Element(tile), pl.Element(d)),
        lambda i, r: (i, pl.multiple_of(jnp.maximum(r * tile - N_META, 0), 8), 0))
    return [x_spec, pl.BlockSpec((N_META, d), lambda i, r: (0, 0))]


def _embed_kernel(x_ref, meta_ref, g_ref, h_ref):
    t = _stream_tile(x_ref, meta_ref, pl.program_id(1))
    h_ref[...] = (t * _rms_scale(t) * g_ref[0]).astype(h_ref.dtype)


def _embed(x, meta, gains):
    b, seq, d = x.shape
    l = seq + N_META
    tile = MIX_TILE
    assert l % tile == 0
    return pl.pallas_call(
        _embed_kernel,
        grid=(b, l // tile),
        in_specs=_stream_specs(tile, d) + [pl.BlockSpec((1, 1, d), lambda i, r: (0, 0, 0))],
        out_specs=pl.BlockSpec((None, tile, d), lambda i, r: (i, r, 0)),
        out_shape=jax.ShapeDtypeStruct((b, l, d), BF16),
        compiler_params=_params(2),
        name="embed_prenorm",
    )(x, meta, gains)


def _inproj_kernel(h_ref, w_ref, *cast_refs):
    n_cast = len(cast_refs) // 2
    o_ref = cast_refs[n_cast]
    for c in range(o_ref.shape[2] // IN_CHUNK):
        cols = slice(c * IN_CHUNK, (c + 1) * IN_CHUNK)
        o_ref[0, :, cols] = jnp.dot(h_ref[0], w_ref[0, :, cols].astype(BF16), preferred_element_type=F32)
    for src, dst in zip(cast_refs[:n_cast], cast_refs[n_cast + 1:]):
        dst[...] = src[0].astype(BF16)


def _inproj(h, w_in, others, layer):
    b, l, d = h.shape
    n = w_in.shape[2]
    steps_per_batch = n // IN_TILE
    assert b * steps_per_batch >= CAST_STEPS

    def chunk(i, j):
        return jnp.minimum(i * steps_per_batch + j, CAST_STEPS - 1)

    cast_in, cast_out, cast_shape = [], [], []
    for w in others:
        rows, cols = w.shape[1:]
        assert rows % (16 * CAST_STEPS) == 0
        cast_in.append(pl.BlockSpec((1, rows // CAST_STEPS, cols), lambda i, j: (layer, chunk(i, j), 0)))
        cast_out.append(pl.BlockSpec((rows // CAST_STEPS, cols), lambda i, j: (chunk(i, j), 0)))
        cast_shape.append(jax.ShapeDtypeStruct((rows, cols), BF16))
    outs = pl.pallas_call(
        _inproj_kernel,
        grid=(b, steps_per_batch),
        in_specs=[pl.BlockSpec((1, l, d), lambda i, j: (i, 0, 0)),
                  pl.BlockSpec((1, d, IN_TILE), lambda i, j: (layer, 0, j))] + cast_in,
        out_specs=[pl.BlockSpec((1, l, IN_TILE), lambda i, j: (i, 0, j))] + cast_out,
        out_shape=[jax.ShapeDtypeStruct((b, l, n), F32)] + cast_shape,
        compiler_params=_params(2),
        name="inproj",
    )(h, w_in, *others)
    return outs[0], outs[1:]


def _softplus2(y):
    return jnp.maximum(y, jnp.log2(1.0 + jnp.exp2(jnp.minimum(y, SOFTPLUS2_CLAMP))))


def _split_bf16(x):
    hi = x.astype(BF16)
    lo = (x - hi.astype(F32)).astype(BF16)
    return jnp.concatenate([hi, lo], axis=1)


def _attn_kernel(q_ref, k_ref, v_ref, g_ref, o_ref, qb_ref, kb_ref, vb_ref, c_ref, acc_ref, *, seq):
    kblk, tq = KEY_BLOCK, Q_TILE
    padded, lanes = qb_ref.shape
    n_tiles = padded // tq
    heads = [slice(h * HEAD_DIM, (h + 1) * HEAD_DIM) for h in range(lanes // HEAD_DIM)]

    qb_ref[:seq, :] = (q_ref[0] * (HEAD_DIM ** -0.5 * LOG2_E)).astype(BF16)
    kb_ref[:seq, :] = k_ref[0].astype(BF16)
    vb_ref[:seq, :] = v_ref[0].astype(BF16)
    if padded > seq:
        for dst in (qb_ref, kb_ref, vb_ref):
            dst[seq:, :] = jnp.zeros((padded - seq, lanes), BF16)

    wr = lax.broadcasted_iota(jnp.int32, (2 * kblk, 2 * kblk), 0) & (kblk - 1)
    wc = lax.broadcasted_iota(jnp.int32, (2 * kblk, 2 * kblk), 1)
    w = jnp.where((wc >= kblk) | (wr >= wc), 1.0, 0.0).astype(BF16)

    row = lax.broadcasted_iota(jnp.int32, (tq, kblk), 0)
    col = lax.broadcasted_iota(jnp.int32, (tq, kblk), 1)

    def scores(q, kk):
        return lax.dot_general(q, kk, (((1,), (1,)), ((), ())), preferred_element_type=F32)

    def window(i):
        q0 = i * tq
        k_lo = max(0, q0 + tq - WINDOW_KEYS)
        nb = pl.cdiv(q0 + tq - k_lo, kblk)
        shift = q0 - k_lo - (nb - 1) * kblk
        assert 0 <= shift and k_lo + nb * kblk <= padded
        return q0, k_lo, nb, col < row + shift

    def stage_scores(i, hd):
        q0, k_lo, nb, _ = window(i)
        return scores(qb_ref[q0:q0 + tq, hd], kb_ref[k_lo:k_lo + nb * kblk, hd])

    def stage_sums(i, hd, y):
        _, _, nb, past = window(i)
        sp = _softplus2(y)
        parts = [_split_bf16(sp[:, b * kblk:(b + 1) * kblk]) for b in range(nb - 1)]
        parts.append(_split_bf16(jnp.where(past, sp[:, (nb - 1) * kblk:], 0.0)))
        return y, jnp.dot(jnp.concatenate(parts, axis=0), w, preferred_element_type=F32)

    def stage_values(i, hd, y, r):
        q0, k_lo, nb, past = window(i)
        later = jnp.zeros((tq, kblk), F32)
        probs = [None] * nb
        for b in reversed(range(nb)):
            rb = r[b * tq:(b + 1) * tq]
            a = jnp.exp2(y[:, b * kblk:(b + 1) * kblk] - rb[:, :kblk] - later)
            probs[b] = (jnp.where(past, a, 0.0) if b == nb - 1 else a).astype(BF16)
            later = later + rb[:, kblk:]
        acc_ref[q0:q0 + tq, hd] = jnp.dot(jnp.concatenate(probs, axis=1),
                                          vb_ref[k_lo:k_lo + nb * kblk, hd], preferred_element_type=F32)
        c_ref[q0:q0 + tq, hd] = later

    units = [(i, hd) for i in range(n_tiles) for hd in heads]
    ys, sums = {}, {}
    for s in range(len(units) + 2 * STAGE_SKEW):
        if s < len(units):
            ys[s] = stage_scores(*units[s])
        if 0 <= s - STAGE_SKEW < len(units):
            sums[s - STAGE_SKEW] = stage_sums(*units[s - STAGE_SKEW], ys.pop(s - STAGE_SKEW))
        if 0 <= s - 2 * STAGE_SKEW < len(units):
            stage_values(*units[s - 2 * STAGE_SKEW], *sums.pop(s - 2 * STAGE_SKEW))

    first_tile = WINDOW_KEYS // tq
    if n_tiles > first_tile:

        def block(hd, q, k0, n_new, c, acc):
            new = col < n_new
            y = scores(q, kb_ref[pl.ds(k0, kblk), hd])
            r = jnp.dot(_split_bf16(jnp.where(new, _softplus2(y), 0.0)), w, preferred_element_type=F32)
            a = jnp.where(new, jnp.exp2(y - r[:, :kblk] - c), 0.0)
            acc = acc + jnp.dot(a.astype(BF16), vb_ref[pl.ds(k0, kblk), hd], preferred_element_type=F32)
            return c + r[:, kblk:], acc

        @pl.when(jnp.min(c_ref[first_tile * tq:, :]) < EXIT_SUM)
        def _():
            def tile(i, _):
                q0 = pl.multiple_of(i * tq, tq)
                k_lo = q0 + tq - WINDOW_KEYS
                n_blocks = (k_lo + kblk - 1) // kblk
                for hd in heads:
                    q = qb_ref[pl.ds(q0, tq), hd]
                    c0 = c_ref[pl.ds(q0, tq), hd]

                    def more(state):
                        return jnp.logical_and(state[0] < n_blocks, state[1] < EXIT_SUM)

                    def step(state, hd=hd, q=q):
                        j, _, c, acc = state
                        top = k_lo - j * kblk
                        k0 = jnp.maximum(top - kblk, 0)
                        c, acc = block(hd, q, pl.multiple_of(k0, tq), top - k0, c, acc)
                        return j + 1, jnp.min(c), c, acc

                    state = (0, jnp.min(c0), c0, acc_ref[pl.ds(q0, tq), hd])
                    acc_ref[pl.ds(q0, tq), hd] = lax.while_loop(more, step, state)[3]
                return 0

            lax.fori_loop(first_tile, n_tiles, tile, 0)

    g = g_ref[0]
    o_ref[0] = (acc_ref[:seq, :] * (g * _sigmoid(g))).astype(o_ref.dtype)


def _attention(zin):
    b, l, _ = zin.shape
    width = HEADS * HEAD_DIM
    lanes = HEAD_GROUP * HEAD_DIM
    groups = HEADS // HEAD_GROUP
    padded = pl.cdiv(l, Q_TILE) * Q_TILE
    assert l % 16 == 0 and WINDOW_KEYS % KEY_BLOCK == 0 and KEY_BLOCK % Q_TILE == 0

    def spec(part):
        return pl.BlockSpec((1, l, lanes), lambda i, h: (i, 0, part * groups + h))

    return pl.pallas_call(
        functools.partial(_attn_kernel, seq=l),
        grid=(b, groups),
        in_specs=[spec(0), spec(1), spec(2), spec(3)],
        out_specs=pl.BlockSpec((1, l, lanes), lambda i, h: (i, 0, h)),
        out_shape=jax.ShapeDtypeStruct((b, l, width), BF16),
        scratch_shapes=[pltpu.VMEM((padded, lanes), BF16)] * 3 + [pltpu.VMEM((padded, lanes), F32)] * 2,
        compiler_params=_params(2),
        name="stickbreak_attn",
    )(zin, zin, zin, zin)


def _mix_kernel(ug_ref, uh_ref, og_ref, m_ref, pw_ref, ps_ref, wa_ref, wp_ref, wo_ref, g_ref,
                *rest, row0, n_stream):
    stream_refs, out_refs, (op_ref, mg_ref) = rest[:n_stream], rest[n_stream:-2], rest[-2:]
    r = pl.program_id(1)
    rows, d = mg_ref.shape
    pw = op_ref.shape[1]
    gd = pw_ref.shape[2]
    u_ref, gp_ref = ug_ref.at[:, :pw], ug_ref.at[:, pw:]
    ma_ref, mp_ref = m_ref.at[:, :d], m_ref.at[:, d:]
    chunks = [slice(n * d // MERGE_SPLIT, (n + 1) * d // MERGE_SPLIT) for n in range(MERGE_SPLIT)]

    halo = uh_ref[...]
    if row0 < HALO:
        halo = jnp.where(r == 0, 0.0, halo)
    ext = jnp.concatenate([halo, u_ref[...]], axis=0)
    pos = row0 + r * rows + lax.broadcasted_iota(jnp.int32, (rows, gd), 0)
    y_attn = []
    for g, win in enumerate(POOL_WINDOWS):
        cols = chunks[g]
        y_attn.append(_sigmoid(ma_ref[:, cols])
                      * jnp.dot(og_ref[...], wa_ref[:, cols], preferred_element_type=F32))
        sl = slice(g * gd, (g + 1) * gd)
        s = ext[:, sl]
        step = 1
        while step < win:
            s = s + pltpu.roll(s, step, axis=0)
            step *= 2
        cnt = jnp.minimum(pos + 1, win).astype(F32)
        pooled = s[HALO:] / cnt - ext[HALO:, sl]
        mixed = jnp.dot(pooled.astype(BF16), pw_ref[0, g], preferred_element_type=F32)
        gate = gp_ref[:, sl]
        op_ref[:, sl] = (mixed * ps_ref[0, :, sl] * (gate * _sigmoid(gate))).astype(BF16)

    for cols, ya in zip(chunks, y_attn):
        y_pool = jnp.dot(op_ref[...], wp_ref[:, cols], preferred_element_type=F32)
        mg_ref[:, cols] = (ya + _sigmoid(mp_ref[:, cols]) * y_pool).astype(BF16)

    x_in = stream_refs[0][...] if n_stream == 1 else _stream_tile(*stream_refs, r)
    x = x_in + jnp.dot(mg_ref[...], wo_ref[...], preferred_element_type=F32)
    normed = x * _rms_scale(x) * g_ref[0]
    if len(out_refs) == 2:
        out_refs[0][...] = x
    out_refs[-1][...] = normed.astype(out_refs[-1].dtype)


def _mix(zin, og, stream, pool_w, pool_scale, w_attn_up, w_pool_up, w_out, gains, layer, last):
    b, l, pw = og.shape
    d = w_out.shape[1]
    u_col, m_col = 4 * pw, 6 * pw
    tile, row0 = (LAST_TILE, N_META) if last else (MIX_TILE, 0)
    out_rows = l - row0
    assert out_rows % tile == 0

    def rows(width, col):
        return pl.BlockSpec((None, pl.Element(tile), pl.Element(width)),
                            lambda i, r: (i, pl.multiple_of(row0 + r * tile, 8), col))

    halo = pl.BlockSpec((None, pl.Element(HALO), pl.Element(pw)),
                        lambda i, r: (i, pl.multiple_of(jnp.maximum(row0 + r * tile - HALO, 0), 8), u_col))

    def layer_of(arr):
        return pl.BlockSpec((1,) + arr.shape[1:], lambda i, r: (layer,) + (0,) * (arr.ndim - 1))

    def whole(arr):
        return pl.BlockSpec(arr.shape, lambda i, r: (0,) * arr.ndim)

    out_spec = pl.BlockSpec((None, tile, d), lambda i, r: (i, r, 0))
    out_shape = [jax.ShapeDtypeStruct((b, out_rows, d), F32)]
    if not last:
        out_shape.append(jax.ShapeDtypeStruct((b, out_rows, d), BF16))
    if isinstance(stream, tuple):
        assert row0 == 0
        stream_specs = _stream_specs(tile, d)
    else:
        stream, stream_specs = (stream,), [rows(d, 0)]
    return pl.pallas_call(
        functools.partial(_mix_kernel, row0=row0, n_stream=len(stream)),
        grid=(b, out_rows // tile),
        in_specs=[rows(2 * pw, u_col), halo, rows(pw, 0), rows(2 * d, m_col),
                  layer_of(pool_w), layer_of(pool_scale), whole(w_attn_up), whole(w_pool_up),
                  whole(w_out),
                  pl.BlockSpec((1, 1, d), lambda i, r: (layer + 1, 0, 0))] + stream_specs,
        out_specs=[out_spec] * len(out_shape),
        out_shape=out_shape,
        scratch_shapes=[pltpu.VMEM((tile, pw), BF16), pltpu.VMEM((tile, d), BF16)],
        compiler_params=_params(2),
        name="mix_out",
    )(zin, zin, og, zin, pool_w, pool_scale, w_attn_up, w_pool_up, w_out, gains, *stream)


def kernel(x, meta_tokens, norm_gain, w_in, pool_w, pool_scale, w_attn_up, w_pool_up, w_out, final_gain):
    depth = norm_gain.shape[0]
    assert meta_tokens.shape[0] == N_META
    gains = jnp.concatenate([norm_gain, final_gain[None]], axis=0)[:, None, :]
    pool_scale3 = pool_scale[:, None, :]
    pool_wb = pool_w.astype(BF16)

    meta = meta_tokens.astype(x.dtype)
    hs, h = (x, meta), _embed(x, meta, gains)
    for layer in range(depth):
        last = layer == depth - 1
        zin, (wab, wpb, wob) = _inproj(h, w_in, (w_attn_up, w_pool_up, w_out), layer)
        og = _attention(zin)
        outs = _mix(zin, og, hs, pool_wb, pool_scale3, wab, wpb, wob, gains, layer, last)
        if last:
            return outs[0]
        hs, h = outs
```

```python
import functools

import jax
import jax.numpy as jnp
from jax import lax
from jax.experimental import pallas as pl
from jax.experimental.pallas import tpu as pltpu

N_META = 16
HEADS = 8
HEAD_DIM = 128
POOL_WINDOWS = (2, 4, 8, 16)
RMS_EPS = 1e-6
LOG2_E = 1.4426950408889634
SOFTPLUS2_CLAMP = 126.0
KEY_BLOCK = 128
Q_TILE = 64
WINDOW_KEYS = 256
EXIT_SUM = 127.0
HEAD_GROUP = 2
STAGE_SKEW = 3
MASKED_SCORE = 1e30
MIX_TILE = 344
LAST_TILE = 256
MERGE_SPLIT = len(POOL_WINDOWS)
HALO = 16
IN_TILE = 1024
IN_CHUNK = 512
CAST_STEPS = 32
VMEM_LIMIT = 56 * 1024 * 1024

BF16 = jnp.bfloat16
F32 = jnp.float32


def _params(n_axes):
    return pltpu.CompilerParams(
        dimension_semantics=("arbitrary",) * n_axes,
        vmem_limit_bytes=VMEM_LIMIT)


def _sigmoid(x):
    return 1.0 / (1.0 + jnp.exp(-x))


def _rms_scale(x):
    return lax.rsqrt(jnp.mean(x * x, axis=-1, keepdims=True) + RMS_EPS)


def _stream_tile(x_ref, meta_ref, tile_index):
    xw = x_ref[...]
    first = jnp.concatenate([meta_ref[...], xw[:-N_META]], axis=0)
    return jnp.where(tile_index == 0, first, xw)


def _stream_specs(tile, d):
    x_spec = pl.BlockSpec(
        (None, pl.Element(tile), pl.Element(d)),
        lambda i, r: (i, pl.multiple_of(jnp.maximum(r * tile - N_META, 0), 8), 0))
    return [x_spec, pl.BlockSpec((N_META, d), lambda i, r: (0, 0))]


def _embed_kernel(x_ref, meta_ref, g_ref, h_ref):
    t = _stream_tile(x_ref, meta_ref, pl.program_id(1))
    h_ref[...] = (t * _rms_scale(t) * g_ref[0]).astype(h_ref.dtype)


def _embed(x, meta, gains):
    b, seq, d = x.shape
    l = seq + N_META
    tile = MIX_TILE
    assert l % tile == 0
    return pl.pallas_call(
        _embed_kernel,
        grid=(b, l // tile),
        in_specs=_stream_specs(tile, d) + [pl.BlockSpec((1, 1, d), lambda i, r: (0, 0, 0))],
        out_specs=pl.BlockSpec((None, tile, d), lambda i, r: (i, r, 0)),
        out_shape=jax.ShapeDtypeStruct((b, l, d), BF16),
        compiler_params=_params(2),
        name="embed_prenorm",
    )(x, meta, gains)


def _inproj_kernel(h_ref, w_ref, *cast_refs):
    n_cast = len(cast_refs) // 2
    o_ref = cast_refs[n_cast]
    for c in range(o_ref.shape[2] // IN_CHUNK):
        cols = slice(c * IN_CHUNK, (c + 1) * IN_CHUNK)
        o_ref[0, :, cols] = jnp.dot(h_ref[0], w_ref[0, :, cols].astype(BF16), preferred_element_type=F32)
    for src, dst in zip(cast_refs[:n_cast], cast_refs[n_cast + 1:]):
        dst[...] = src[0].astype(BF16)


def _inproj(h, w_in, others, layer):
    b, l, d = h.shape
    n = w_in.shape[2]
    steps_per_batch = n // IN_TILE
    assert b * steps_per_batch >= CAST_STEPS

    def chunk(i, j):
        return jnp.minimum(i * steps_per_batch + j, CAST_STEPS - 1)

    cast_in, cast_out, cast_shape = [], [], []
    for w in others:
        rows, cols = w.shape[1:]
        assert rows % (16 * CAST_STEPS) == 0
        cast_in.append(pl.BlockSpec((1, rows // CAST_STEPS, cols), lambda i, j: (layer, chunk(i, j), 0)))
        cast_out.append(pl.BlockSpec((rows // CAST_STEPS, cols), lambda i, j: (chunk(i, j), 0)))
        cast_shape.append(jax.ShapeDtypeStruct((rows, cols), BF16))
    outs = pl.pallas_call(
        _inproj_kernel,
        grid=(b, steps_per_batch),
        in_specs=[pl.BlockSpec((1, l, d), lambda i, j: (i, 0, 0)),
                  pl.BlockSpec((1, d, IN_TILE), lambda i, j: (layer, 0, j))] + cast_in,
        out_specs=[pl.BlockSpec((1, l, IN_TILE), lambda i, j: (i, 0, j))] + cast_out,
        out_shape=[jax.ShapeDtypeStruct((b, l, n), F32)] + cast_shape,
        compiler_params=_params(2),
        name="inproj",
    )(h, w_in, *others)
    return outs[0], outs[1:]


def _softplus2(y):
    return jnp.maximum(y, jnp.log2(1.0 + jnp.exp2(jnp.minimum(y, SOFTPLUS2_CLAMP))))


def _split_bf16(x):
    hi = x.astype(BF16)
    lo = (x - hi.astype(F32)).astype(BF16)
    return jnp.concatenate([hi, lo], axis=1)


def _attn_kernel(q_ref, k_ref, v_ref, g_ref, o_ref, qb_ref, kb_ref, vb_ref, c_ref, acc_ref, *, seq):
    kblk, tq = KEY_BLOCK, Q_TILE
    padded, lanes = qb_ref.shape
    n_tiles = padded // tq
    heads = [slice(h * HEAD_DIM, (h + 1) * HEAD_DIM) for h in range(lanes // HEAD_DIM)]

    qb_ref[:seq, :] = (q_ref[0] * (HEAD_DIM ** -0.5 * LOG2_E)).astype(BF16)
    kb_ref[:seq, :] = k_ref[0].astype(BF16)
    vb_ref[:seq, :] = v_ref[0].astype(BF16)
    if padded > seq:
        for dst in (qb_ref, kb_ref, vb_ref):
            dst[seq:, :] = jnp.zeros((padded - seq, lanes), BF16)

    wr = lax.broadcasted_iota(jnp.int32, (2 * kblk, 2 * kblk), 0) & (kblk - 1)
    wc = lax.broadcasted_iota(jnp.int32, (2 * kblk, 2 * kblk), 1)
    w = jnp.where((wc >= kblk) | (wr >= wc), 1.0, 0.0).astype(BF16)

    row = lax.broadcasted_iota(jnp.int32, (tq, kblk), 0)
    col = lax.broadcasted_iota(jnp.int32, (tq, kblk), 1)

    def scores(q, kk):
        return lax.dot_general(q, kk, (((1,), (1,)), ((), ())), preferred_element_type=F32)

    def window(i):
        q0 = i * tq
        k_lo = max(0, q0 + tq - WINDOW_KEYS)
        nb = pl.cdiv(q0 + tq - k_lo, kblk)
        shift = q0 - k_lo - (nb - 1) * kblk
        assert 0 <= shift and k_lo + nb * kblk <= padded
        return q0, k_lo, nb, col < row + shift

    def stage_scores(i):
        q0, k_lo, nb, _ = window(i)
        return [scores(qb_ref[q0:q0 + tq, hd], kb_ref[k_lo:k_lo + nb * kblk, hd]) for hd in heads]

    def stage_sums(i, ys):
        _, _, nb, past = window(i)
        masked, parts = [], []
        for y in ys:
            blocks = [y[:, b * kblk:(b + 1) * kblk] for b in range(nb - 1)]
            blocks.append(jnp.where(past, y[:, (nb - 1) * kblk:], -MASKED_SCORE))
            masked.append(blocks)
            parts += [_split_bf16(_softplus2(yb)) for yb in blocks]
        return masked, jnp.dot(jnp.concatenate(parts, axis=0), w, preferred_element_type=F32)

    def stage_values(i, masked, r):
        q0, k_lo, nb, _ = window(i)
        for n, (hd, blocks) in enumerate(zip(heads, masked)):
            later = jnp.zeros((tq, kblk), F32)
            probs = [None] * nb
            for b in reversed(range(nb)):
                rb = r[(n * nb + b) * tq:(n * nb + b + 1) * tq]
                probs[b] = jnp.exp2(blocks[b] - rb[:, :kblk] - later).astype(BF16)
                later = later + rb[:, kblk:]
            acc_ref[q0:q0 + tq, hd] = jnp.dot(jnp.concatenate(probs, axis=1),
                                              vb_ref[k_lo:k_lo + nb * kblk, hd], preferred_element_type=F32)
            c_ref[q0:q0 + tq, hd] = later

    ys, sums = {}, {}
    for s in range(n_tiles + 2 * STAGE_SKEW):
        if s < n_tiles:
            ys[s] = stage_scores(s)
        if 0 <= s - STAGE_SKEW < n_tiles:
            sums[s - STAGE_SKEW] = stage_sums(s - STAGE_SKEW, ys.pop(s - STAGE_SKEW))
        if 0 <= s - 2 * STAGE_SKEW < n_tiles:
            stage_values(s - 2 * STAGE_SKEW, *sums.pop(s - 2 * STAGE_SKEW))

    first_tile = WINDOW_KEYS // tq
    if n_tiles > first_tile:

        def block(hd, q, k0, n_new, c, acc):
            new = col < n_new
            y = scores(q, kb_ref[pl.ds(k0, kblk), hd])
            r = jnp.dot(_split_bf16(jnp.where(new, _softplus2(y), 0.0)), w, preferred_element_type=F32)
            a = jnp.where(new, jnp.exp2(y - r[:, :kblk] - c), 0.0)
            acc = acc + jnp.dot(a.astype(BF16), vb_ref[pl.ds(k0, kblk), hd], preferred_element_type=F32)
            return c + r[:, kblk:], acc

        @pl.when(jnp.min(c_ref[first_tile * tq:, :]) < EXIT_SUM)
        def _():
            def tile(i, _):
                q0 = pl.multiple_of(i * tq, tq)
                k_lo = q0 + tq - WINDOW_KEYS
                n_blocks = (k_lo + kblk - 1) // kblk
                for hd in heads:
                    q = qb_ref[pl.ds(q0, tq), hd]
                    c0 = c_ref[pl.ds(q0, tq), hd]

                    def more(state):
                        return jnp.logical_and(state[0] < n_blocks, state[1] < EXIT_SUM)

                    def step(state, hd=hd, q=q):
                        j, _, c, acc = state
                        top = k_lo - j * kblk
                        k0 = jnp.maximum(top - kblk, 0)
                        c, acc = block(hd, q, pl.multiple_of(k0, tq), top - k0, c, acc)
                        return j + 1, jnp.min(c), c, acc

                    state = (0, jnp.min(c0), c0, acc_ref[pl.ds(q0, tq), hd])
                    acc_ref[pl.ds(q0, tq), hd] = lax.while_loop(more, step, state)[3]
                return 0

            lax.fori_loop(first_tile, n_tiles, tile, 0)

    g = g_ref[0]
    o_ref[0] = (acc_ref[:seq, :] * (g * _sigmoid(g))).astype(o_ref.dtype)


def _attention(zin):
    b, l, _ = zin.shape
    width = HEADS * HEAD_DIM
    lanes = HEAD_GROUP * HEAD_DIM
    groups = HEADS // HEAD_GROUP
    padded = pl.cdiv(l, Q_TILE) * Q_TILE
    assert l % 16 == 0 and WINDOW_KEYS % KEY_BLOCK == 0 and KEY_BLOCK % Q_TILE == 0

    def spec(part):
        return pl.BlockSpec((1, l, lanes), lambda i, h: (i, 0, part * groups + h))

    return pl.pallas_call(
        functools.partial(_attn_kernel, seq=l),
        grid=(b, groups),
        in_specs=[spec(0), spec(1), spec(2), spec(3)],
        out_specs=pl.BlockSpec((1, l, lanes), lambda i, h: (i, 0, h)),
        out_shape=jax.ShapeDtypeStruct((b, l, width), BF16),
        scratch_shapes=[pltpu.VMEM((padded, lanes), BF16)] * 3 + [pltpu.VMEM((padded, lanes), F32)] * 2,
        compiler_params=_params(2),
        name="stickbreak_attn",
    )(zin, zin, zin, zin)


def _mix_kernel(ug_ref, uh_ref, og_ref, m_ref, pw_ref, ps_ref, wa_ref, wp_ref, wo_ref, g_ref,
                *rest, row0, n_stream):
    stream_refs, out_refs, (op_ref, mg_ref) = rest[:n_stream], rest[n_stream:-2], rest[-2:]
    r = pl.program_id(1)
    rows, d = mg_ref.shape
    pw = op_ref.shape[1]
    gd = pw_ref.shape[2]
    u_ref, gp_ref = ug_ref.at[:, :pw], ug_ref.at[:, pw:]
    ma_ref, mp_ref = m_ref.at[:, :d], m_ref.at[:, d:]
    chunks = [slice(n * d // MERGE_SPLIT, (n + 1) * d // MERGE_SPLIT) for n in range(MERGE_SPLIT)]

    halo = uh_ref[...]
    if row0 < HALO:
        halo = jnp.where(r == 0, 0.0, halo)
    ext = jnp.concatenate([halo, u_ref[...]], axis=0)
    pos = row0 + r * rows + lax.broadcasted_iota(jnp.int32, (rows, gd), 0)
    y_attn = []
    for g, win in enumerate(POOL_WINDOWS):
        cols = chunks[g]
        y_attn.append(_sigmoid(ma_ref[:, cols])
                      * jnp.dot(og_ref[...], wa_ref[:, cols], preferred_element_type=F32))
        sl = slice(g * gd, (g + 1) * gd)
        s = ext[:, sl]
        step = 1
        while step < win:
            s = s + pltpu.roll(s, step, axis=0)
            step *= 2
        cnt = jnp.minimum(pos + 1, win).astype(F32)
        pooled = s[HALO:] / cnt - ext[HALO:, sl]
        mixed = jnp.dot(pooled.astype(BF16), pw_ref[0, g], preferred_element_type=F32)
        gate = gp_ref[:, sl]
        op_ref[:, sl] = (mixed * ps_ref[0, :, sl] * (gate * _sigmoid(gate))).astype(BF16)

    for cols, ya in zip(chunks, y_attn):
        y_pool = jnp.dot(op_ref[...], wp_ref[:, cols], preferred_element_type=F32)
        mg_ref[:, cols] = (ya + _sigmoid(mp_ref[:, cols]) * y_pool).astype(BF16)

    x_in = stream_refs[0][...] if n_stream == 1 else _stream_tile(*stream_refs, r)
    x = x_in + jnp.dot(mg_ref[...], wo_ref[...], preferred_element_type=F32)
    normed = x * _rms_scale(x) * g_ref[0]
    if len(out_refs) == 2:
        out_refs[0][...] = x
    out_refs[-1][...] = normed.astype(out_refs[-1].dtype)


def _mix(zin, og, stream, pool_w, pool_scale, w_attn_up, w_pool_up, w_out, gains, layer, last):
    b, l, pw = og.shape
    d = w_out.shape[1]
    u_col, m_col = 4 * pw, 6 * pw
    tile, row0 = (LAST_TILE, N_META) if last else (MIX_TILE, 0)
    out_rows = l - row0
    assert out_rows % tile == 0

    def rows(width, col):
        return pl.BlockSpec((None, pl.Element(tile), pl.Element(width)),
                            lambda i, r: (i, pl.multiple_of(row0 + r * tile, 8), col))

    halo = pl.BlockSpec((None, pl.Element(HALO), pl.Element(pw)),
                        lambda i, r: (i, pl.multiple_of(jnp.maximum(row0 + r * tile - HALO, 0), 8), u_col))

    def layer_of(arr):
        return pl.BlockSpec((1,) + arr.shape[1:], lambda i, r: (layer,) + (0,) * (arr.ndim - 1))

    def whole(arr):
        return pl.BlockSpec(arr.shape, lambda i, r: (0,) * arr.ndim)

    out_spec = pl.BlockSpec((None, tile, d), lambda i, r: (i, r, 0))
    out_shape = [jax.ShapeDtypeStruct((b, out_rows, d), F32)]
    if not last:
        out_shape.append(jax.ShapeDtypeStruct((b, out_rows, d), BF16))
    if isinstance(stream, tuple):
        assert row0 == 0
        stream_specs = _stream_specs(tile, d)
    else:
        stream, stream_specs = (stream,), [rows(d, 0)]
    return pl.pallas_call(
        functools.partial(_mix_kernel, row0=row0, n_stream=len(stream)),
        grid=(b, out_rows // tile),
        in_specs=[rows(2 * pw, u_col), halo, rows(pw, 0), rows(2 * d, m_col),
                  layer_of(pool_w), layer_of(pool_scale), whole(w_attn_up), whole(w_pool_up),
                  whole(w_out),
                  pl.BlockSpec((1, 1, d), lambda i, r: (layer + 1, 0, 0))] + stream_specs,
        out_specs=[out_spec] * len(out_shape),
        out_shape=out_shape,
        scratch_shapes=[pltpu.VMEM((tile, pw), BF16), pltpu.VMEM((tile, d), BF16)],
        compiler_params=_params(2),
        name="mix_out",
    )(zin, zin, og, zin, pool_w, pool_scale, w_attn_up, w_pool_up, w_out, gains, *stream)


def kernel(x, meta_tokens, norm_gain, w_in, pool_w, pool_scale, w_attn_up, w_pool_up, w_out, final_gain):
    depth = norm_gain.shape[0]
    assert meta_tokens.shape[0] == N_META
    gains = jnp.concatenate([norm_gain, final_gain[None]], axis=0)[:, None, :]
    pool_scale3 = pool_scale[:, None, :]
    pool_wb = pool_w.astype(BF16)

    meta = meta_tokens.astype(x.dtype)
    hs, h = (x, meta), _embed(x, meta, gains)
    for layer in range(depth):
        last = layer == depth - 1
        zin, (wab, wpb, wob) = _inproj(h, w_in, (w_attn_up, w_pool_up, w_out), layer)
        og = _attention(zin)
        outs = _mix(zin, og, hs, pool_wb, pool_scale3, wab, wpb, wob, gains, layer, last)
        if last:
            return outs[0]
        hs, h = outs
```

```python
import functools

import jax
import jax.numpy as jnp
from jax import lax
from jax.experimental import pallas as pl
from jax.experimental.pallas import tpu as pltpu

N_META = 16
HEADS = 8
HEAD_DIM = 128
POOL_WINDOWS = (2, 4, 8, 16)
RMS_EPS = 1e-6
LOG2_E = 1.4426950408889634
SOFTPLUS2_CLAMP = 126.0
KEY_BLOCK = 128
Q_TILE = 64
WINDOW_KEYS = 256
EXIT_SUM = 127.0
HEAD_GROUP = 2
STAGE_SKEW = 6
MIX_TILE = 344
LAST_TILE = 256
MERGE_SPLIT = len(POOL_WINDOWS)
HALO = 16
IN_TILE = 1024
IN_CHUNK = 512
CAST_STEPS = 32
VMEM_LIMIT = 56 * 1024 * 1024

BF16 = jnp.bfloat16
F32 = jnp.float32


def _params(n_axes):
    return pltpu.CompilerParams(
        dimension_semantics=("arbitrary",) * n_axes,
        vmem_limit_bytes=VMEM_LIMIT)


def _sigmoid(x):
    return 1.0 / (1.0 + jnp.exp(-x))


def _rms_scale(x):
    return lax.rsqrt(jnp.mean(x * x, axis=-1, keepdims=True) + RMS_EPS)


def _stream_tile(x_ref, meta_ref, tile_index):
    xw = x_ref[...]
    first = jnp.concatenate([meta_ref[...], xw[:-N_META]], axis=0)
    return jnp.where(tile_index == 0, first, xw)


def _stream_specs(tile, d):
    x_spec = pl.BlockSpec(
        (None, pl.Element(tile), pl.Element(d)),
        lambda i, r: (i, pl.multiple_of(jnp.maximum(r * tile - N_META, 0), 8), 0))
    return [x_spec, pl.BlockSpec((N_META, d), lambda i, r: (0, 0))]


def _embed_kernel(x_ref, meta_ref, g_ref, h_ref):
    t = _stream_tile(x_ref, meta_ref, pl.program_id(1))
    h_ref[...] = (t * _rms_scale(t) * g_ref[0]).astype(h_ref.dtype)


def _embed(x, meta, gains):
    b, seq, d = x.shape
    l = seq + N_META
    tile = MIX_TILE
    assert l % tile == 0
    return pl.pallas_call(
        _embed_kernel,
        grid=(b, l // tile),
        in_specs=_stream_specs(tile, d) + [pl.BlockSpec((1, 1, d), lambda i, r: (0, 0, 0))],
        out_specs=pl.BlockSpec((None, tile, d), lambda i, r: (i, r, 0)),
        out_shape=jax.ShapeDtypeStruct((b, l, d), BF16),
        compiler_params=_params(2),
        name="embed_prenorm",
    )(x, meta, gains)


def _inproj_kernel(h_ref, w_ref, *cast_refs):
    n_cast = len(cast_refs) // 2
    o_ref = cast_refs[n_cast]
    for c in range(o_ref.shape[2] // IN_CHUNK):
        cols = slice(c * IN_CHUNK, (c + 1) * IN_CHUNK)
        o_ref[0, :, cols] = jnp.dot(h_ref[0], w_ref[0, :, cols].astype(BF16), preferred_element_type=F32)
    for src, dst in zip(cast_refs[:n_cast], cast_refs[n_cast + 1:]):
        dst[...] = src[0].astype(BF16)


def _inproj(h, w_in, others, layer):
    b, l, d = h.shape
    n = w_in.shape[2]
    steps_per_batch = n // IN_TILE
    assert b * steps_per_batch >= CAST_STEPS

    def chunk(i, j):
        return jnp.minimum(i * steps_per_batch + j, CAST_STEPS - 1)

    cast_in, cast_out, cast_shape = [], [], []
    for w in others:
        rows, cols = w.shape[1:]
        assert rows % (16 * CAST_STEPS) == 0
        cast_in.append(pl.BlockSpec((1, rows // CAST_STEPS, cols), lambda i, j: (layer, chunk(i, j), 0)))
        cast_out.append(pl.BlockSpec((rows // CAST_STEPS, cols), lambda i, j: (chunk(i, j), 0)))
        cast_shape.append(jax.ShapeDtypeStruct((rows, cols), BF16))
    outs = pl.pallas_call(
        _inproj_kernel,
        grid=(b, steps_per_batch),
        in_specs=[pl.BlockSpec((1, l, d), lambda i, j: (i, 0, 0)),
                  pl.BlockSpec((1, d, IN_TILE), lambda i, j: (layer, 0, j))] + cast_in,
        out_specs=[pl.BlockSpec((1, l, IN_TILE), lambda i, j: (i, 0, j))] + cast_out,
        out_shape=[jax.ShapeDtypeStruct((b, l, n), F32)] + cast_shape,
        compiler_params=_params(2),
        name="inproj",
    )(h, w_in, *others)
    return outs[0], outs[1:]


def _softplus2(y):
    return jnp.maximum(y, jnp.log2(1.0 + jnp.exp2(jnp.minimum(y, SOFTPLUS2_CLAMP))))


def _split_bf16(x):
    hi = x.astype(BF16)
    lo = (x - hi.astype(F32)).astype(BF16)
    return jnp.concatenate([hi, lo], axis=1)


def _attn_kernel(q_ref, k_ref, v_ref, g_ref, o_ref, qb_ref, kb_ref, vb_ref, c_ref, acc_ref, *, seq):
    kblk, tq = KEY_BLOCK, Q_TILE
    padded, lanes = qb_ref.shape
    n_tiles = padded // tq
    heads = [slice(h * HEAD_DIM, (h + 1) * HEAD_DIM) for h in range(lanes // HEAD_DIM)]

    qb_ref[:seq, :] = (q_ref[0] * (HEAD_DIM ** -0.5 * LOG2_E)).astype(BF16)
    kb_ref[:seq, :] = k_ref[0].astype(BF16)
    vb_ref[:seq, :] = v_ref[0].astype(BF16)
    if padded > seq:
        for dst in (qb_ref, kb_ref, vb_ref):
            dst[seq:, :] = jnp.zeros((padded - seq, lanes), BF16)

    wr = lax.broadcasted_iota(jnp.int32, (2 * kblk, 2 * kblk), 0) & (kblk - 1)
    wc = lax.broadcasted_iota(jnp.int32, (2 * kblk, 2 * kblk), 1)
    w = jnp.where((wc >= kblk) | (wr >= wc), 1.0, 0.0).astype(BF16)

    row = lax.broadcasted_iota(jnp.int32, (tq, kblk), 0)
    col = lax.broadcasted_iota(jnp.int32, (tq, kblk), 1)

    def scores(q, kk):
        return lax.dot_general(q, kk, (((1,), (1,)), ((), ())), preferred_element_type=F32)

    def window(i):
        q0 = i * tq
        k_lo = max(0, q0 + tq - WINDOW_KEYS)
        nb = pl.cdiv(q0 + tq - k_lo, kblk)
        shift = q0 - k_lo - (nb - 1) * kblk
        assert 0 <= shift and k_lo + nb * kblk <= padded
        return q0, k_lo, nb, col < row + shift

    def stage_scores(i, hd):
        q0, k_lo, nb, _ = window(i)
        return scores(qb_ref[q0:q0 + tq, hd], kb_ref[k_lo:k_lo + nb * kblk, hd])

    def stage_sums(i, hd, y):
        _, _, nb, past = window(i)
        sp = _softplus2(y)
        parts = [_split_bf16(sp[:, b * kblk:(b + 1) * kblk]) for b in range(nb - 1)]
        parts.append(_split_bf16(jnp.where(past, sp[:, (nb - 1) * kblk:], 0.0)))
        return y, jnp.dot(jnp.concatenate(parts, axis=0), w, preferred_element_type=F32)

    def stage_values(i, hd, y, r):
        q0, k_lo, nb, past = window(i)
        later = jnp.zeros((tq, kblk), F32)
        probs = [None] * nb
        for b in reversed(range(nb)):
            rb = r[b * tq:(b + 1) * tq]
            a = jnp.exp2(y[:, b * kblk:(b + 1) * kblk] - rb[:, :kblk] - later)
            probs[b] = (jnp.where(past, a, 0.0) if b == nb - 1 else a).astype(BF16)
            later = later + rb[:, kblk:]
        acc_ref[q0:q0 + tq, hd] = jnp.dot(jnp.concatenate(probs, axis=1),
                                          vb_ref[k_lo:k_lo + nb * kblk, hd], preferred_element_type=F32)
        c_ref[q0:q0 + tq, hd] = later

    units = [(i, hd) for i in range(n_tiles) for hd in heads]
    ys, sums = {}, {}
    for s in range(len(units) + 2 * STAGE_SKEW):
        if s < len(units):
            ys[s] = stage_scores(*units[s])
        if 0 <= s - STAGE_SKEW < len(units):
            sums[s - STAGE_SKEW] = stage_sums(*units[s - STAGE_SKEW], ys.pop(s - STAGE_SKEW))
        if 0 <= s - 2 * STAGE_SKEW < len(units):
            stage_values(*units[s - 2 * STAGE_SKEW], *sums.pop(s - 2 * STAGE_SKEW))

    first_tile = WINDOW_KEYS // tq
    if n_tiles > first_tile:

        def block(hd, q, k0, n_new, c, acc):
            new = col < n_new
            y = scores(q, kb_ref[pl.ds(k0, kblk), hd])
            r = jnp.dot(_split_bf16(jnp.where(new, _softplus2(y), 0.0)), w, preferred_element_type=F32)
            a = jnp.where(new, jnp.exp2(y - r[:, :kblk] - c), 0.0)
            acc = acc + jnp.dot(a.astype(BF16), vb_ref[pl.ds(k0, kblk), hd], preferred_element_type=F32)
            return c + r[:, kblk:], acc

        @pl.when(jnp.min(c_ref[first_tile * tq:, :]) < EXIT_SUM)
        def _():
            def tile(i, _):
                q0 = pl.multiple_of(i * tq, tq)
                k_lo = q0 + tq - WINDOW_KEYS
                n_blocks = (k_lo + kblk - 1) // kblk
                for hd in heads:
                    q = qb_ref[pl.ds(q0, tq), hd]
                    c0 = c_ref[pl.ds(q0, tq), hd]

                    def more(state):
                        return jnp.logical_and(state[0] < n_blocks, state[1] < EXIT_SUM)

                    def step(state, hd=hd, q=q):
                        j, _, c, acc = state
                        top = k_lo - j * kblk
                        k0 = jnp.maximum(top - kblk, 0)
                        c, acc = block(hd, q, pl.multiple_of(k0, tq), top - k0, c, acc)
                        return j + 1, jnp.min(c), c, acc

                    state = (0, jnp.min(c0), c0, acc_ref[pl.ds(q0, tq), hd])
                    acc_ref[pl.ds(q0, tq), hd] = lax.while_loop(more, step, state)[3]
                return 0

            lax.fori_loop(first_tile, n_tiles, tile, 0)

    g = g_ref[0]
    o_ref[0] = (acc_ref[:seq, :] * (g * _sigmoid(g))).astype(o_ref.dtype)


def _attention(zin):
    b, l, _ = zin.shape
    width = HEADS * HEAD_DIM
    lanes = HEAD_GROUP * HEAD_DIM
    groups = HEADS // HEAD_GROUP
    padded = pl.cdiv(l, Q_TILE) * Q_TILE
    assert l % 16 == 0 and WINDOW_KEYS % KEY_BLOCK == 0 and KEY_BLOCK % Q_TILE == 0

    def spec(part):
        return pl.BlockSpec((1, l, lanes), lambda i, h: (i, 0, part * groups + h))

    return pl.pallas_call(
        functools.partial(_attn_kernel, seq=l),
        grid=(b, groups),
        in_specs=[spec(0), spec(1), spec(2), spec(3)],
        out_specs=pl.BlockSpec((1, l, lanes), lambda i, h: (i, 0, h)),
        out_shape=jax.ShapeDtypeStruct((b, l, width), BF16),
        scratch_shapes=[pltpu.VMEM((padded, lanes), BF16)] * 3 + [pltpu.VMEM((padded, lanes), F32)] * 2,
        compiler_params=_params(2),
        name="stickbreak_attn",
    )(zin, zin, zin, zin)


def _mix_kernel(ug_ref, uh_ref, og_ref, m_ref, pw_ref, ps_ref, wa_ref, wp_ref, wo_ref, g_ref,
                *rest, row0, n_stream):
    stream_refs, out_refs, (op_ref, mg_ref) = rest[:n_stream], rest[n_stream:-2], rest[-2:]
    r = pl.program_id(1)
    rows, d = mg_ref.shape
    pw = op_ref.shape[1]
    gd = pw_ref.shape[2]
    u_ref, gp_ref = ug_ref.at[:, :pw], ug_ref.at[:, pw:]
    ma_ref, mp_ref = m_ref.at[:, :d], m_ref.at[:, d:]
    chunks = [slice(n * d // MERGE_SPLIT, (n + 1) * d // MERGE_SPLIT) for n in range(MERGE_SPLIT)]

    halo = uh_ref[...]
    if row0 < HALO:
        halo = jnp.where(r == 0, 0.0, halo)
    ext = jnp.concatenate([halo, u_ref[...]], axis=0)
    pos = row0 + r * rows + lax.broadcasted_iota(jnp.int32, (rows, gd), 0)
    y_attn = []
    for g, win in enumerate(POOL_WINDOWS):
        cols = chunks[g]
        y_attn.append(_sigmoid(ma_ref[:, cols])
                      * jnp.dot(og_ref[...], wa_ref[:, cols], preferred_element_type=F32))
        sl = slice(g * gd, (g + 1) * gd)
        s = ext[:, sl]
        step = 1
        while step < win:
            s = s + pltpu.roll(s, step, axis=0)
            step *= 2
        cnt = jnp.minimum(pos + 1, win).astype(F32)
        pooled = s[HALO:] / cnt - ext[HALO:, sl]
        mixed = jnp.dot(pooled.astype(BF16), pw_ref[0, g], preferred_element_type=F32)
        gate = gp_ref[:, sl]
        op_ref[:, sl] = (mixed * ps_ref[0, :, sl] * (gate * _sigmoid(gate))).astype(BF16)

    for cols, ya in zip(chunks, y_attn):
        y_pool = jnp.dot(op_ref[...], wp_ref[:, cols], preferred_element_type=F32)
        mg_ref[:, cols] = (ya + _sigmoid(mp_ref[:, cols]) * y_pool).astype(BF16)

    x_in = stream_refs[0][...] if n_stream == 1 else _stream_tile(*stream_refs, r)
    x = x_in + jnp.dot(mg_ref[...], wo_ref[...], preferred_element_type=F32)
    normed = x * _rms_scale(x) * g_ref[0]
    if len(out_refs) == 2:
        out_refs[0][...] = x
    out_refs[-1][...] = normed.astype(out_refs[-1].dtype)


def _mix(zin, og, stream, pool_w, pool_scale, w_attn_up, w_pool_up, w_out, gains, layer, last):
    b, l, pw = og.shape
    d = w_out.shape[1]
    u_col, m_col = 4 * pw, 6 * pw
    tile, row0 = (LAST_TILE, N_META) if last else (MIX_TILE, 0)
    out_rows = l - row0
    assert out_rows % tile == 0

    def rows(width, col):
        return pl.BlockSpec((None, pl.Element(tile), pl.Element(width)),
                            lambda i, r: (i, pl.multiple_of(row0 + r * tile, 8), col))

    halo = pl.BlockSpec((None, pl.Element(HALO), pl.Element(pw)),
                        lambda i, r: (i, pl.multiple_of(jnp.maximum(row0 + r * tile - HALO, 0), 8), u_col))

    def layer_of(arr):
        return pl.BlockSpec((1,) + arr.shape[1:], lambda i, r: (layer,) + (0,) * (arr.ndim - 1))

    def whole(arr):
        return pl.BlockSpec(arr.shape, lambda i, r: (0,) * arr.ndim)

    out_spec = pl.BlockSpec((None, tile, d), lambda i, r: (i, r, 0))
    out_shape = [jax.ShapeDtypeStruct((b, out_rows, d), F32)]
    if not last:
        out_shape.append(jax.ShapeDtypeStruct((b, out_rows, d), BF16))
    if isinstance(stream, tuple):
        assert row0 == 0
        stream_specs = _stream_specs(tile, d)
    else:
        stream, stream_specs = (stream,), [rows(d, 0)]
    return pl.pallas_call(
        functools.partial(_mix_kernel, row0=row0, n_stream=len(stream)),
        grid=(b, out_rows // tile),
        in_specs=[rows(2 * pw, u_col), halo, rows(pw, 0), rows(2 * d, m_col),
                  layer_of(pool_w), layer_of(pool_scale), whole(w_attn_up), whole(w_pool_up),
                  whole(w_out),
                  pl.BlockSpec((1, 1, d), lambda i, r: (layer + 1, 0, 0))] + stream_specs,
        out_specs=[out_spec] * len(out_shape),
        out_shape=out_shape,
        scratch_shapes=[pltpu.VMEM((tile, pw), BF16), pltpu.VMEM((tile, d), BF16)],
        compiler_params=_params(2),
        name="mix_out",
    )(zin, zin, og, zin, pool_w, pool_scale, w_attn_up, w_pool_up, w_out, gains, *stream)


def kernel(x, meta_tokens, norm_gain, w_in, pool_w, pool_scale, w_attn_up, w_pool_up, w_out, final_gain):
    depth = norm_gain.shape[0]
    assert meta_tokens.shape[0] == N_META
    gains = jnp.concatenate([norm_gain, final_gain[None]], axis=0)[:, None, :]
    pool_scale3 = pool_scale[:, None, :]
    pool_wb = pool_w.astype(BF16)

    meta = meta_tokens.astype(x.dtype)
    hs, h = (x, meta), _embed(x, meta, gains)
    for layer in range(depth):
        last = layer == depth - 1
        zin, (wab, wpb, wob) = _inproj(h, w_in, (w_attn_up, w_pool_up, w_out), layer)
        og = _attention(zin)
        outs = _mix(zin, og, hs, pool_wb, pool_scale3, wab, wpb, wob, gains, layer, last)
        if last:
            return outs[0]
        hs, h = outs
```

```python
import functools

import jax
import jax.numpy as jnp
from jax import lax
from jax.experimental import pallas as pl
from jax.experimental.pallas import tpu as pltpu

N_META = 16
HEADS = 8
HEAD_DIM = 128
POOL_WINDOWS = (2, 4, 8, 16)
RMS_EPS = 1e-6
LOG2_E = 1.4426950408889634
SOFTPLUS2_CLAMP = 126.0
KEY_BLOCK = 128
Q_TILE = 64
WINDOW_KEYS = 256
EXIT_SUM = 127.0
HEAD_GROUP = 4
STAGE_SKEW = 6
MIX_TILE = 344
LAST_TILE = 256
MERGE_SPLIT = len(POOL_WINDOWS)
HALO = 16
IN_TILE = 1024
IN_CHUNK = 512
CAST_STEPS = 32
VMEM_LIMIT = 56 * 1024 * 1024

BF16 = jnp.bfloat16
F32 = jnp.float32


def _params(n_axes):
    return pltpu.CompilerParams(
        dimension_semantics=("arbitrary",) * n_axes,
        vmem_limit_bytes=VMEM_LIMIT)


def _sigmoid(x):
    return 1.0 / (1.0 + jnp.exp(-x))


def _rms_scale(x):
    return lax.rsqrt(jnp.mean(x * x, axis=-1, keepdims=True) + RMS_EPS)


def _stream_tile(x_ref, meta_ref, tile_index):
    xw = x_ref[...]
    first = jnp.concatenate([meta_ref[...], xw[:-N_META]], axis=0)
    return jnp.where(tile_index == 0, first, xw)


def _stream_specs(tile, d):
    x_spec = pl.BlockSpec(
        (None, pl.Element(tile), pl.Element(d)),
        lambda i, r: (i, pl.multiple_of(jnp.maximum(r * tile - N_META, 0), 8), 0))
    return [x_spec, pl.BlockSpec((N_META, d), lambda i, r: (0, 0))]


def _embed_kernel(x_ref, meta_ref, g_ref, h_ref):
    t = _stream_tile(x_ref, meta_ref, pl.program_id(1))
    h_ref[...] = (t * _rms_scale(t) * g_ref[0]).astype(h_ref.dtype)


def _embed(x, meta, gains):
    b, seq, d = x.shape
    l = seq + N_META
    tile = MIX_TILE
    assert l % tile == 0
    return pl.pallas_call(
        _embed_kernel,
        grid=(b, l // tile),
        in_specs=_stream_specs(tile, d) + [pl.BlockSpec((1, 1, d), lambda i, r: (0, 0, 0))],
        out_specs=pl.BlockSpec((None, tile, d), lambda i, r: (i, r, 0)),
        out_shape=jax.ShapeDtypeStruct((b, l, d), BF16),
        compiler_params=_params(2),
        name="embed_prenorm",
    )(x, meta, gains)


def _inproj_kernel(h_ref, w_ref, *cast_refs):
    n_cast = len(cast_refs) // 2
    o_ref = cast_refs[n_cast]
    for c in range(o_ref.shape[2] // IN_CHUNK):
        cols = slice(c * IN_CHUNK, (c + 1) * IN_CHUNK)
        o_ref[0, :, cols] = jnp.dot(h_ref[0], w_ref[0, :, cols].astype(BF16), preferred_element_type=F32)
    for src, dst in zip(cast_refs[:n_cast], cast_refs[n_cast + 1:]):
        dst[...] = src[0].astype(BF16)


def _inproj(h, w_in, others, layer):
    b, l, d = h.shape
    n = w_in.shape[2]
    steps_per_batch = n // IN_TILE
    assert b * steps_per_batch >= CAST_STEPS

    def chunk(i, j):
        return jnp.minimum(i * steps_per_batch + j, CAST_STEPS - 1)

    cast_in, cast_out, cast_shape = [], [], []
    for w in others:
        rows, cols = w.shape[1:]
        assert rows % (16 * CAST_STEPS) == 0
        cast_in.append(pl.BlockSpec((1, rows // CAST_STEPS, cols), lambda i, j: (layer, chunk(i, j), 0)))
        cast_out.append(pl.BlockSpec((rows // CAST_STEPS, cols), lambda i, j: (chunk(i, j), 0)))
        cast_shape.append(jax.ShapeDtypeStruct((rows, cols), BF16))
    outs = pl.pallas_call(
        _inproj_kernel,
        grid=(b, steps_per_batch),
        in_specs=[pl.BlockSpec((1, l, d), lambda i, j: (i, 0, 0)),
                  pl.BlockSpec((1, d, IN_TILE), lambda i, j: (layer, 0, j))] + cast_in,
        out_specs=[pl.BlockSpec((1, l, IN_TILE), lambda i, j: (i, 0, j))] + cast_out,
        out_shape=[jax.ShapeDtypeStruct((b, l, n), F32)] + cast_shape,
        compiler_params=_params(2),
        name="inproj",
    )(h, w_in, *others)
    return outs[0], outs[1:]


def _softplus2(y):
    return jnp.maximum(y, jnp.log2(1.0 + jnp.exp2(jnp.minimum(y, SOFTPLUS2_CLAMP))))


def _split_bf16(x):
    hi = x.astype(BF16)
    lo = (x - hi.astype(F32)).astype(BF16)
    return jnp.concatenate([hi, lo], axis=1)


def _attn_kernel(q_ref, k_ref, v_ref, g_ref, o_ref, qb_ref, kb_ref, vb_ref, c_ref, acc_ref, *, seq):
    kblk, tq = KEY_BLOCK, Q_TILE
    padded, lanes = qb_ref.shape
    n_tiles = padded // tq
    heads = [slice(h * HEAD_DIM, (h + 1) * HEAD_DIM) for h in range(lanes // HEAD_DIM)]

    qb_ref[:seq, :] = (q_ref[0] * (HEAD_DIM ** -0.5 * LOG2_E)).astype(BF16)
    kb_ref[:seq, :] = k_ref[0].astype(BF16)
    vb_ref[:seq, :] = v_ref[0].astype(BF16)
    if padded > seq:
        for dst in (qb_ref, kb_ref, vb_ref):
            dst[seq:, :] = jnp.zeros((padded - seq, lanes), BF16)

    wr = lax.broadcasted_iota(jnp.int32, (2 * kblk, 2 * kblk), 0) & (kblk - 1)
    wc = lax.broadcasted_iota(jnp.int32, (2 * kblk, 2 * kblk), 1)
    w = jnp.where((wc >= kblk) | (wr >= wc), 1.0, 0.0).astype(BF16)

    row = lax.broadcasted_iota(jnp.int32, (tq, kblk), 0)
    col = lax.broadcasted_iota(jnp.int32, (tq, kblk), 1)

    def scores(q, kk):
        return lax.dot_general(q, kk, (((1,), (1,)), ((), ())), preferred_element_type=F32)

    def window(i):
        q0 = i * tq
        k_lo = max(0, q0 + tq - WINDOW_KEYS)
        nb = pl.cdiv(q0 + tq - k_lo, kblk)
        shift = q0 - k_lo - (nb - 1) * kblk
        assert 0 <= shift and k_lo + nb * kblk <= padded
        return q0, k_lo, nb, col < row + shift

    def stage_scores(i, hd):
        q0, k_lo, nb, _ = window(i)
        return scores(qb_ref[q0:q0 + tq, hd], kb_ref[k_lo:k_lo + nb * kblk, hd])

    def stage_sums(i, hd, y):
        _, _, nb, past = window(i)
        sp = _softplus2(y)
        parts = [_split_bf16(sp[:, b * kblk:(b + 1) * kblk]) for b in range(nb - 1)]
        parts.append(_split_bf16(jnp.where(past, sp[:, (nb - 1) * kblk:], 0.0)))
        return y, jnp.dot(jnp.concatenate(parts, axis=0), w, preferred_element_type=F32)

    def stage_values(i, hd, y, r):
        q0, k_lo, nb, past = window(i)
        later = jnp.zeros((tq, kblk), F32)
        probs = [None] * nb
        for b in reversed(range(nb)):
            rb = r[b * tq:(b + 1) * tq]
            a = jnp.exp2(y[:, b * kblk:(b + 1) * kblk] - rb[:, :kblk] - later)
            probs[b] = (jnp.where(past, a, 0.0) if b == nb - 1 else a).astype(BF16)
            later = later + rb[:, kblk:]
        acc_ref[q0:q0 + tq, hd] = jnp.dot(jnp.concatenate(probs, axis=1),
                                          vb_ref[k_lo:k_lo + nb * kblk, hd], preferred_element_type=F32)
        c_ref[q0:q0 + tq, hd] = later

    units = [(i, hd) for i in range(n_tiles) for hd in heads]
    ys, sums = {}, {}
    for s in range(len(units) + 2 * STAGE_SKEW):
        if s < len(units):
            ys[s] = stage_scores(*units[s])
        if 0 <= s - STAGE_SKEW < len(units):
            sums[s - STAGE_SKEW] = stage_sums(*units[s - STAGE_SKEW], ys.pop(s - STAGE_SKEW))
        if 0 <= s - 2 * STAGE_SKEW < len(units):
            stage_values(*units[s - 2 * STAGE_SKEW], *sums.pop(s - 2 * STAGE_SKEW))

    first_tile = WINDOW_KEYS // tq
    if n_tiles > first_tile:

        def block(hd, q, k0, n_new, c, acc):
            new = col < n_new
            y = scores(q, kb_ref[pl.ds(k0, kblk), hd])
            r = jnp.dot(_split_bf16(jnp.where(new, _softplus2(y), 0.0)), w, preferred_element_type=F32)
            a = jnp.where(new, jnp.exp2(y - r[:, :kblk] - c), 0.0)
            acc = acc + jnp.dot(a.astype(BF16), vb_ref[pl.ds(k0, kblk), hd], preferred_element_type=F32)
            return c + r[:, kblk:], acc

        @pl.when(jnp.min(c_ref[first_tile * tq:, :]) < EXIT_SUM)
        def _():
            def tile(i, _):
                q0 = pl.multiple_of(i * tq, tq)
                k_lo = q0 + tq - WINDOW_KEYS
                n_blocks = (k_lo + kblk - 1) // kblk
                for hd in heads:
                    q = qb_ref[pl.ds(q0, tq), hd]
                    c0 = c_ref[pl.ds(q0, tq), hd]

                    def more(state):
                        return jnp.logical_and(state[0] < n_blocks, state[1] < EXIT_SUM)

                    def step(state, hd=hd, q=q):
                        j, _, c, acc = state
                        top = k_lo - j * kblk
                        k0 = jnp.maximum(top - kblk, 0)
                        c, acc = block(hd, q, pl.multiple_of(k0, tq), top - k0, c, acc)
                        return j + 1, jnp.min(c), c, acc

                    state = (0, jnp.min(c0), c0, acc_ref[pl.ds(q0, tq), hd])
                    acc_ref[pl.ds(q0, tq), hd] = lax.while_loop(more, step, state)[3]
                return 0

            lax.fori_loop(first_tile, n_tiles, tile, 0)

    g = g_ref[0]
    o_ref[0] = (acc_ref[:seq, :] * (g * _sigmoid(g))).astype(o_ref.dtype)


def _attention(zin):
    b, l, _ = zin.shape
    width = HEADS * HEAD_DIM
    lanes = HEAD_GROUP * HEAD_DIM
    groups = HEADS // HEAD_GROUP
    padded = pl.cdiv(l, Q_TILE) * Q_TILE
    assert l % 16 == 0 and WINDOW_KEYS % KEY_BLOCK == 0 and KEY_BLOCK % Q_TILE == 0

    def spec(part):
        return pl.BlockSpec((1, l, lanes), lambda i, h: (i, 0, part * groups + h))

    return pl.pallas_call(
        functools.partial(_attn_kernel, seq=l),
        grid=(b, groups),
        in_specs=[spec(0), spec(1), spec(2), spec(3)],
        out_specs=pl.BlockSpec((1, l, lanes), lambda i, h: (i, 0, h)),
        out_shape=jax.ShapeDtypeStruct((b, l, width), BF16),
        scratch_shapes=[pltpu.VMEM((padded, lanes), BF16)] * 3 + [pltpu.VMEM((padded, lanes), F32)] * 2,
        compiler_params=_params(2),
        name="stickbreak_attn",
    )(zin, zin, zin, zin)


def _mix_kernel(ug_ref, uh_ref, og_ref, m_ref, pw_ref, ps_ref, wa_ref, wp_ref, wo_ref, g_ref,
                *rest, row0, n_stream):
    stream_refs, out_refs, (op_ref, mg_ref) = rest[:n_stream], rest[n_stream:-2], rest[-2:]
    r = pl.program_id(1)
    rows, d = mg_ref.shape
    pw = op_ref.shape[1]
    gd = pw_ref.shape[2]
    u_ref, gp_ref = ug_ref.at[:, :pw], ug_ref.at[:, pw:]
    ma_ref, mp_ref = m_ref.at[:, :d], m_ref.at[:, d:]
    chunks = [slice(n * d // MERGE_SPLIT, (n + 1) * d // MERGE_SPLIT) for n in range(MERGE_SPLIT)]

    halo = uh_ref[...]
    if row0 < HALO:
        halo = jnp.where(r == 0, 0.0, halo)
    ext = jnp.concatenate([halo, u_ref[...]], axis=0)
    pos = row0 + r * rows + lax.broadcasted_iota(jnp.int32, (rows, gd), 0)
    y_attn = []
    for g, win in enumerate(POOL_WINDOWS):
        cols = chunks[g]
        y_attn.append(_sigmoid(ma_ref[:, cols])
                      * jnp.dot(og_ref[...], wa_ref[:, cols], preferred_element_type=F32))
        sl = slice(g * gd, (g + 1) * gd)
        s = ext[:, sl]
        step = 1
        while step < win:
            s = s + pltpu.roll(s, step, axis=0)
            step *= 2
        cnt = jnp.minimum(pos + 1, win).astype(F32)
        pooled = s[HALO:] / cnt - ext[HALO:, sl]
        mixed = jnp.dot(pooled.astype(BF16), pw_ref[0, g], preferred_element_type=F32)
        gate = gp_ref[:, sl]
        op_ref[:, sl] = (mixed * ps_ref[0, :, sl] * (gate * _sigmoid(gate))).astype(BF16)

    for cols, ya in zip(chunks, y_attn):
        y_pool = jnp.dot(op_ref[...], wp_ref[:, cols], preferred_element_type=F32)
        mg_ref[:, cols] = (ya + _sigmoid(mp_ref[:, cols]) * y_pool).astype(BF16)

    x_in = stream_refs[0][...] if n_stream == 1 else _stream_tile(*stream_refs, r)
    x = x_in + jnp.dot(mg_ref[...], wo_ref[...], preferred_element_type=F32)
    normed = x * _rms_scale(x) * g_ref[0]
    if len(out_refs) == 2:
        out_refs[0][...] = x
    out_refs[-1][...] = normed.astype(out_refs[-1].dtype)


def _mix(zin, og, stream, pool_w, pool_scale, w_attn_up, w_pool_up, w_out, gains, layer, last):
    b, l, pw = og.shape
    d = w_out.shape[1]
    u_col, m_col = 4 * pw, 6 * pw
    tile, row0 = (LAST_TILE, N_META) if last else (MIX_TILE, 0)
    out_rows = l - row0
    assert out_rows % tile == 0

    def rows(width, col):
        return pl.BlockSpec((None, pl.Element(tile), pl.Element(width)),
                            lambda i, r: (i, pl.multiple_of(row0 + r * tile, 8), col))

    halo = pl.BlockSpec((None, pl.Element(HALO), pl.Element(pw)),
                        lambda i, r: (i, pl.multiple_of(jnp.maximum(row0 + r * tile - HALO, 0), 8), u_col))

    def layer_of(arr):
        return pl.BlockSpec((1,) + arr.shape[1:], lambda i, r: (layer,) + (0,) * (arr.ndim - 1))

    def whole(arr):
        return pl.BlockSpec(arr.shape, lambda i, r: (0,) * arr.ndim)

    out_spec = pl.BlockSpec((None, tile, d), lambda i, r: (i, r, 0))
    out_shape = [jax.ShapeDtypeStruct((b, out_rows, d), F32)]
    if not last:
        out_shape.append(jax.ShapeDtypeStruct((b, out_rows, d), BF16))
    if isinstance(stream, tuple):
        assert row0 == 0
        stream_specs = _stream_specs(tile, d)
    else:
        stream, stream_specs = (stream,), [rows(d, 0)]
    return pl.pallas_call(
        functools.partial(_mix_kernel, row0=row0, n_stream=len(stream)),
        grid=(b, out_rows // tile),
        in_specs=[rows(2 * pw, u_col), halo, rows(pw, 0), rows(2 * d, m_col),
                  layer_of(pool_w), layer_of(pool_scale), whole(w_attn_up), whole(w_pool_up),
                  whole(w_out),
                  pl.BlockSpec((1, 1, d), lambda i, r: (layer + 1, 0, 0))] + stream_specs,
        out_specs=[out_spec] * len(out_shape),
        out_shape=out_shape,
        scratch_shapes=[pltpu.VMEM((tile, pw), BF16), pltpu.VMEM((tile, d), BF16)],
        compiler_params=_params(2),
        name="mix_out",
    )(zin, zin, og, zin, pool_w, pool_scale, w_attn_up, w_pool_up, w_out, gains, *stream)


def kernel(x, meta_tokens, norm_gain, w_in, pool_w, pool_scale, w_attn_up, w_pool_up, w_out, final_gain):
    depth = norm_gain.shape[0]
    assert meta_tokens.shape[0] == N_META
    gains = jnp.concatenate([norm_gain, final_gain[None]], axis=0)[:, None, :]
    pool_scale3 = pool_scale[:, None, :]
    pool_wb = pool_w.astype(BF16)

    meta = meta_tokens.astype(x.dtype)
    hs, h = (x, meta), _embed(x, meta, gains)
    for layer in range(depth):
        last = layer == depth - 1
        zin, (wab, wpb, wob) = _inproj(h, w_in, (w_attn_up, w_pool_up, w_out), layer)
        og = _attention(zin)
        outs = _mix(zin, og, hs, pool_wb, pool_scale3, wab, wpb, wob, gains, layer, last)
        if last:
            return outs[0]
        hs, h = outs
```
